```python
import jax, jax.numpy as jnp
from jax import lax
import numpy as np

D_MODEL = 1024
BATCH = 1
SEQ = 16384
DEPTH = 1
DEC_BATCH = 32
DEC_SEQ = 4
PAST_LEN = 16384
PAGE_SIZE = 128

D_MIX = D_MODEL
D_HGRN = D_MIX // 2
D_ATT = D_MIX - D_HGRN
H_HGRN = 4
DK_HGRN = D_HGRN // H_HGRN
DV_HGRN = D_HGRN // H_HGRN
H_ATT = 4
D_HEAD = D_ATT // H_ATT
H_IDX = 8
D_IDX = 64
TOPK_MAX = 256
HGRN_CHUNK = 64
Q_BLOCK = 128
ALPHA = (2 * DEPTH) ** 0.25
BETA = (8 * DEPTH) ** -0.25
EPS = 1e-5
SPLIT_SIZES = (D_HGRN, D_HGRN, D_HGRN, D_HGRN,
               D_ATT, D_ATT, D_ATT, D_ATT,
               H_IDX * D_IDX, D_IDX, H_IDX)
D_IN = sum(SPLIT_SIZES)
VALUE_SLOTS = (2, 6)

kernel_name = "hymba_hgrn2_dsa_deepnorm_step"

F32 = jnp.float32


def layer_norm(x, g, b):
    xf = x.astype(F32)
    mu = jnp.mean(xf, axis=-1, keepdims=True)
    var = jnp.mean(jnp.square(xf - mu), axis=-1, keepdims=True)
    return (xf - mu) * lax.rsqrt(var + EPS) * g.astype(F32) + b.astype(F32)


def rms_norm_heads(o, g):
    of = o.astype(F32)
    of = of * lax.rsqrt(jnp.mean(jnp.square(of), axis=-1, keepdims=True) + EPS)
    return of.reshape(*o.shape[:-2], -1) * g.astype(F32)


def project(x, w_in, lb):
    B, T, _ = x.shape
    z = x @ w_in
    points = np.cumsum(SPLIT_SIZES)[:-1].tolist()
    hq, hf, hi, hg, aq, ak, av, ag, iq, ik, iw = jnp.split(z, points, axis=-1)
    heads = lambda a, h: a.reshape(B, T, h, -1)
    f = lb + (1.0 - lb) * jax.nn.sigmoid(hf.astype(F32))
    hgrn = (heads(jax.nn.silu(hq), H_HGRN), heads(1.0 - f, H_HGRN), heads(hi, H_HGRN),
            heads(jnp.log(f), H_HGRN), hg)
    att = (heads(aq, H_ATT), heads(ak, H_ATT), heads(av, H_ATT), ag)
    idx = (heads(iq, H_IDX), ik, iw)
    return hgrn, att, idx


def hgrn_chunked(q, k, v, logf, s0):
    B, T, H, _ = q.shape
    DV = v.shape[-1]
    C = HGRN_CHUNK if T % HGRN_CHUNK == 0 else T
    nc = T // C

    def chunks(a):
        return jnp.moveaxis(a.astype(F32).reshape(B, nc, C, H, a.shape[-1]), 1, 0)

    causal = jnp.tril(jnp.ones((C, C), dtype=bool))

    def step(S, xs):
        qc, kc, vc, gc = xs
        b = jnp.cumsum(gc, axis=1)
        b_end = b[:, -1]
        qd = qc * jnp.exp(b)
        kd = kc * jnp.exp(-b)
        att = jnp.where(causal, jnp.einsum('bthk,bshk->bhts', qd, kd), 0.0)
        o = jnp.einsum('bthk,bhkv->bthv', qd, S) + jnp.einsum('bhts,bshv->bthv', att, vc)
        S = jnp.exp(b_end)[..., None] * S + jnp.einsum(
            'bshk,bshv->bhkv', kc * jnp.exp(b_end[:, None] - b), vc)
        return S, o

    S, o = lax.scan(step, s0.astype(F32), (chunks(q), chunks(k), chunks(v), chunks(logf)))
    o = jnp.moveaxis(o, 0, 1).reshape(B, T, H, DV)
    return o, S


def indexer_scores(qi, wi, ki):
    s = jnp.einsum('bqhd,bsd->bqhs', qi.astype(F32), ki.astype(F32))
    return jnp.einsum('bqhs,bqh->bqs', jax.nn.relu(s), wi.astype(F32)) * (H_IDX ** -0.5 * D_IDX ** -0.5)


def select_keys(scores, q_pos, n_sel):
    key_pos = jnp.arange(scores.shape[-1])
    allowed = key_pos[None, :] <= q_pos[:, None]
    scores = jnp.where(allowed[None], scores, -jnp.inf)
    _, idx = lax.top_k(scores, n_sel)
    valid = idx <= q_pos[None, :, None]
    return idx, valid


def sparse_attend(q, k_sel, v_sel, valid):
    logits = jnp.einsum('bqhd,bqkhd->bhqk', q.astype(F32), k_sel.astype(F32)) * (D_HEAD ** -0.5)
    logits = jnp.where(valid[:, None], logits, -jnp.inf)
    p = jax.nn.softmax(logits, axis=-1)
    return jnp.einsum('bhqk,bqkhd->bqhd', p, v_sel.astype(F32))


def dsa_prompt(q, k, v, qi, ki, wi):
    B, T = q.shape[:2]
    n_sel = min(TOPK_MAX, T // 4)
    nb = T // Q_BLOCK

    def blocks(a):
        return jnp.moveaxis(a.reshape(B, nb, Q_BLOCK, *a.shape[2:]), 1, 0)

    gather = jax.vmap(lambda arr, ix: arr[ix])
    starts = jnp.arange(nb, dtype=jnp.int32) * Q_BLOCK

    def one_block(args):
        qb, qib, wib, s0 = args
        q_pos = s0 + jnp.arange(Q_BLOCK, dtype=jnp.int32)
        idx, valid = select_keys(indexer_scores(qib, wib, ki), q_pos, n_sel)
        return sparse_attend(qb, gather(k, idx), gather(v, idx), valid)

    out = lax.map(one_block, (blocks(q), blocks(qi), blocks(wi), starts))
    return jnp.moveaxis(out, 0, 1).reshape(B, T, H_ATT, D_HEAD)


def dsa_sample(q, k_new, v_new, qi, ki_new, wi, cache_k, cache_v, cache_ki, page_table):
    B, T = q.shape[:2]
    past = page_table.shape[1] * PAGE_SIZE
    n_sel = min(TOPK_MAX, (past + T) // 4)
    ki_past = cache_ki[page_table].reshape(B, past, D_IDX)
    ki_all = jnp.concatenate([ki_past.astype(ki_new.dtype), ki_new], axis=1)
    q_pos = past + jnp.arange(T, dtype=jnp.int32)
    idx, valid = select_keys(indexer_scores(qi, wi, ki_all), q_pos, n_sel)
    flat = idx.reshape(B, T * n_sel)
    is_past = flat < past
    pidx = jnp.minimum(flat, past - 1)
    phys = jnp.take_along_axis(page_table, pidx // PAGE_SIZE, axis=1)
    off = pidx % PAGE_SIZE
    nidx = jnp.clip(flat - past, 0, T - 1)

    def rows(cache, new):
        from_past = cache[phys, off].astype(F32)
        from_new = jnp.take_along_axis(new, nidx[:, :, None, None], axis=1).astype(F32)
        sel = jnp.where(is_past[:, :, None, None], from_past, from_new)
        return sel.reshape(B, T, n_sel, H_ATT, D_HEAD)

    return sparse_attend(q, rows(cache_k, k_new), rows(cache_v, v_new), valid)


def merge(x, o_h, gate_h, o_a, gate_a, g_h, g_a, w_out, ln_g, ln_b):
    hh = rms_norm_heads(o_h, g_h) * jax.nn.silu(gate_h.astype(F32))
    aa = rms_norm_heads(o_a, g_a) * jax.nn.silu(gate_a.astype(F32))
    mix = jnp.concatenate([hh, aa], axis=-1).astype(x.dtype) @ w_out
    return layer_norm(ALPHA * x + mix, ln_g, ln_b).astype(x.dtype)


def setup_inputs(seed: int = 0) -> dict:
    key = jax.random.key(seed)
    ks = jax.random.split(key, 16)
    n_pages = PAST_LEN // PAGE_SIZE
    n_pool = (DEC_BATCH * n_pages * 5 + 3) // 4
    nrm = lambda k, shape: jax.random.normal(k, shape, dtype=F32)
    col_scale = np.concatenate([np.full((s,), BETA if i in VALUE_SLOTS else 1.0, dtype=np.float32)
                                for i, s in enumerate(SPLIT_SIZES)])
    w_in = nrm(ks[0], (DEPTH, D_MODEL, D_IN)) * (D_MODEL ** -0.5) * jnp.asarray(col_scale)
    w_out = nrm(ks[1], (DEPTH, D_MIX, D_MODEL)) * (D_MIX ** -0.5) * BETA
    page_table = jax.random.permutation(ks[2], n_pool)[: DEC_BATCH * n_pages].reshape(
        DEC_BATCH, n_pages).astype(jnp.int32)
    return {
        "x_prompt": nrm(ks[3], (BATCH, SEQ, D_MODEL)),
        "x_sample": nrm(ks[4], (DEC_BATCH, DEC_SEQ, D_MODEL)),
        "cache_k": nrm(ks[5], (DEPTH, n_pool, PAGE_SIZE, H_ATT, D_HEAD)),
        "cache_v": nrm(ks[6], (DEPTH, n_pool, PAGE_SIZE, H_ATT, D_HEAD)) * BETA,
        "cache_kidx": nrm(ks[7], (DEPTH, n_pool, PAGE_SIZE, D_IDX)),
        "state_hgrn": nrm(ks[8], (DEPTH, DEC_BATCH, H_HGRN, DK_HGRN, DV_HGRN)) * 0.5,
        "page_table": page_table,
        "w_in": w_in,
        "w_out": w_out,
        "lb_logits": nrm(ks[9], (DEPTH + 1, D_HGRN)) * 0.1,
        "g_hgrn": 1.0 + 0.01 * nrm(ks[10], (DEPTH, D_HGRN)),
        "g_attn": 1.0 + 0.01 * nrm(ks[11], (DEPTH, D_ATT)),
        "ln_g": 1.0 + 0.01 * nrm(ks[12], (DEPTH, D_MODEL)),
        "ln_b": 0.01 * nrm(ks[13], (DEPTH, D_MODEL)),
    }


def reference(x_prompt, x_sample, cache_k, cache_v, cache_kidx, state_hgrn, page_table,
              w_in, w_out, lb_logits, g_hgrn, g_attn, ln_g, ln_b):
    lb_all = jnp.cumsum(jax.nn.softmax(lb_logits.astype(F32), axis=0), axis=0)
    hp, hs = x_prompt, x_sample
    kp, vp, kip, sp, ksm, vsm, kism, ssm = [], [], [], [], [], [], [], []
    for l in range(DEPTH):
        lb = lb_all[l]
        (q, k, v, logf, gh), (aq, ak, av, ag), (iq, ik, iw) = project(hp, w_in[l], lb)
        s0 = jnp.zeros((hp.shape[0], H_HGRN, DK_HGRN, DV_HGRN), dtype=F32)
        o_h, s_p = hgrn_chunked(q, k, v, logf, s0)
        o_a = dsa_prompt(aq, ak, av, iq, ik, iw)
        hp_next = merge(hp, o_h, gh, o_a, ag, g_hgrn[l], g_attn[l], w_out[l], ln_g[l], ln_b[l])
        kp.append(ak); vp.append(av); kip.append(ik); sp.append(s_p.astype(state_hgrn.dtype))
        (q, k, v, logf, gh), (aq, ak, av, ag), (iq, ik, iw) = project(hs, w_in[l], lb)
        o_h, s_s = hgrn_chunked(q, k, v, logf, state_hgrn[l])
        o_a = dsa_sample(aq, ak, av, iq, ik, iw, cache_k[l], cache_v[l], cache_kidx[l], page_table)
        hs_next = merge(hs, o_h, gh, o_a, ag, g_hgrn[l], g_attn[l], w_out[l], ln_g[l], ln_b[l])
        ksm.append(ak); vsm.append(av); kism.append(ik); ssm.append(s_s.astype(state_hgrn.dtype))
        hp, hs = hp_next, hs_next
    return (hp, hs, jnp.stack(kp), jnp.stack(vp), jnp.stack(kip), jnp.stack(sp),
            jnp.stack(ksm), jnp.stack(vsm), jnp.stack(kism), jnp.stack(ssm))
```

```python
import functools

import jax
import jax.numpy as jnp
from jax import lax
from jax.experimental import pallas as pl
from jax.experimental.pallas import tpu as pltpu

F32 = jnp.float32
BF16 = jnp.bfloat16
I32 = jnp.int32

H_HGRN = 4
H_ATT = 4
D_HEAD = 128
H_IDX = 8
D_IDX = 64
SEG = 512
N_SEG = 9
TOPK_MAX = 256
HGRN_CHUNK = 64
Q_BLOCK = 128
EPS = 1e-5
DEPTH = 1
ALPHA = (2 * DEPTH) ** 0.25
IDX_SCALE = H_IDX ** -0.5 * D_IDX ** -0.5
ATT_SCALE = D_HEAD ** -0.5

LANES = 128
SUBLANES = 8
SAMPLE_ROWS = SUBLANES

INT_MIN = -(2 ** 31)
INT_MAX = 2 ** 31 - 1
NEG_BIG = -1e30
MAX_BISECT_STEPS = 34

NT_DIMS = (((1,), (1,)), ((), ()))
TN_DIMS = (((0,), (0,)), ((), ()))


def _sigmoid(x):
    return 1.0 / (1.0 + jnp.exp(-x))


def _vmem_params(nbytes, semantics):
    return pltpu.CompilerParams(dimension_semantics=semantics, vmem_limit_bytes=int(nbytes))


def _resident(shape):
    zeros = (0,) * len(shape)
    return pl.BlockSpec(shape, lambda *_: zeros, pipeline_mode=pl.Buffered(1))


def _proj_kernel(x_ref, wm_ref, wt_ref, wikt_ref, lbl_ref,
                 q_ref, kk_ref, v_ref, lf_ref, gh_ref, aq_ref, ak_ref, av_ref, ag_ref, iq_ref,
                 ik_ref, iw_ref, kbf_ref, vbf_ref, *maybe_kit_ref, kc):
    x = x_ref[...].astype(BF16)

    def seg(j):
        return jnp.dot(x, wm_ref[:, j * SEG:(j + 1) * SEG], preferred_element_type=F32)

    hq = seg(0)
    q_ref[...] = hq * _sigmoid(hq)
    lbl = lbl_ref[...]
    e = jnp.exp(lbl - jnp.max(lbl, axis=0, keepdims=True))
    lb = e[0:1] / jnp.sum(e, axis=0, keepdims=True)
    f = lb + (1.0 - lb) * _sigmoid(seg(1))
    kk_ref[...] = 1.0 - f
    lf_ref[...] = jnp.log(f)
    v_ref[...] = seg(2)
    gh_ref[...] = seg(3)
    aq_ref[...] = seg(4)
    ak = seg(5)
    ak_ref[...] = ak
    kbf_ref[...] = ak.astype(BF16)
    av = seg(6)
    av_ref[...] = av
    vbf_ref[...] = av.astype(BF16)
    ag_ref[...] = seg(7)
    iq_ref[...] = seg(8)
    tail = jnp.dot(x, wt_ref[...], preferred_element_type=F32)
    ik_ref[...] = tail[:, :D_IDX]
    iw_ref[...] = tail[:, D_IDX:D_IDX + H_IDX]
    if maybe_kit_ref:
        (kit_ref,) = maybe_kit_ref
        kt = lax.dot_general(wikt_ref[...], x, NT_DIMS, preferred_element_type=F32).astype(BF16)
        for j in range(kt.shape[1] // kc):
            kit_ref[j] = kt[:, j * kc:(j + 1) * kc]


def _project(x2, wm, wt, wikt, lbl, *, tm, kc):
    t, d = x2.shape
    assert t % tm == 0 and (kc == 0 or tm % kc == 0)
    row = lambda w: pl.BlockSpec((tm, w), lambda i: (i, 0))
    out_shape = [jax.ShapeDtypeStruct((t, SEG), F32)] * 10 + [
        jax.ShapeDtypeStruct((t, D_IDX), F32), jax.ShapeDtypeStruct((t, H_IDX), F32),
        jax.ShapeDtypeStruct((t, SEG), BF16), jax.ShapeDtypeStruct((t, SEG), BF16)]
    out_specs = [row(SEG)] * 10 + [row(D_IDX), row(H_IDX), row(SEG), row(SEG)]
    if kc:
        out_shape.append(jax.ShapeDtypeStruct((t // kc, D_IDX, kc), BF16))
        out_specs.append(pl.BlockSpec((tm // kc, D_IDX, kc), lambda i: (i, 0, 0)))
    vmem = wm.size * 2 + 2 * tm * d * 4 + 2 * tm * SEG * (10 * 4 + 2 * 2) + (8 << 20)
    return pl.pallas_call(
        functools.partial(_proj_kernel, kc=kc),
        grid=(t // tm,),
        in_specs=[row(d), _resident(wm.shape), _resident(wt.shape), _resident(wikt.shape),
                  _resident(lbl.shape)],
        out_specs=out_specs,
        out_shape=out_shape,
        compiler_params=_vmem_params(vmem, ("parallel",)),
        name="proj",
    )(x2, wm, wt, wikt, lbl)


def _hgrn_kernel(q_ref, kk_ref, v_ref, lf_ref, s0_ref, o_ref, sout_ref, st_ref, *, rows, t_valid):
    c_len = HGRN_CHUNK
    step = pl.program_id(1)

    @pl.when(step == 0)
    def _():
        for h in range(H_HGRN):
            st_ref[h] = s0_ref[0, h].T

    r = lax.broadcasted_iota(I32, (c_len, c_len), 0)
    c = lax.broadcasted_iota(I32, (c_len, c_len), 1)
    causal = r >= c
    tri = causal.astype(F32)
    pad = c_len - rows if rows < c_len else 0

    def load(ref, r0, n, h):
        a = ref[0, r0:r0 + n, h * D_HEAD:(h + 1) * D_HEAD]
        if pad:
            a = jnp.concatenate([a, jnp.zeros((pad, D_HEAD), F32)], axis=0)
        return a

    n_rows = min(rows, c_len)
    for ci in range(max(rows // c_len, 1)):
        r0 = ci * c_len
        for h in range(H_HGRN):
            q = load(q_ref, r0, n_rows, h)
            kk = load(kk_ref, r0, n_rows, h)
            v = load(v_ref, r0, n_rows, h)
            lf = load(lf_ref, r0, n_rows, h)
            if t_valid < c_len:
                lf = jnp.where(lax.broadcasted_iota(I32, lf.shape, 0) < t_valid, lf, 0.0)
            b = jnp.dot(tri, lf, precision=lax.Precision.HIGHEST, preferred_element_type=F32)
            b_end = b[c_len - 1:c_len, :]
            qd = (q * jnp.exp(b)).astype(BF16)
            kd = (kk * jnp.exp(-b)).astype(BF16)
            k2 = (kk * jnp.exp(b_end - b)).astype(BF16)
            vb = v.astype(BF16)
            att = lax.dot_general(qd, kd, NT_DIMS, preferred_element_type=F32)
            att = jnp.where(causal, att, 0.0).astype(BF16)
            st = st_ref[h]
            o = lax.dot_general(qd, st.astype(BF16), NT_DIMS, preferred_element_type=F32)
            o = o + jnp.dot(att, vb, preferred_element_type=F32)
            st_ref[h] = st * jnp.exp(b_end) + lax.dot_general(vb, k2, TN_DIMS, preferred_element_type=F32)
            o_ref[0, r0:r0 + n_rows, h * D_HEAD:(h + 1) * D_HEAD] = o[:n_rows]

    @pl.when(step == pl.num_programs(1) - 1)
    def _():
        for h in range(H_HGRN):
            sout_ref[0, h] = st_ref[h].T


def _hgrn(q, kk, v, lf, s0, *, rows, t_valid):
    b, t, _ = q.shape
    assert t % rows == 0
    blk = pl.BlockSpec((1, rows, SEG), lambda i, j: (i, j, 0))
    sblk = pl.BlockSpec((1, H_HGRN, D_HEAD, D_HEAD), lambda i, j: (i, 0, 0, 0))
    return pl.pallas_call(
        functools.partial(_hgrn_kernel, rows=rows, t_valid=t_valid),
        grid=(b, t // rows),
        in_specs=[blk, blk, blk, blk, sblk],
        out_specs=[blk, sblk],
        out_shape=[jax.ShapeDtypeStruct((b, t, SEG), F32),
                   jax.ShapeDtypeStruct((b, H_HGRN, D_HEAD, D_HEAD), F32)],
        scratch_shapes=[pltpu.VMEM((H_HGRN, D_HEAD, D_HEAD), F32)],
        compiler_params=_vmem_params(32 << 20, ("parallel", "arbitrary")),
        name="hgrn",
    )(q, kk, v, lf, s0)


def _sortable_key(x):
    b = lax.bitcast_convert_type(jnp.where(x == 0.0, 0.0, x), I32)
    return b ^ (lax.shift_right_arithmetic(b, 31) & INT_MAX)


def _lane_sum(x):
    return jnp.broadcast_to(jnp.sum(x, axis=1, keepdims=True), x.shape)


def _topk_threshold(keys_ref, nch, nsel, rows, kc):
    nt = kc // LANES
    shape = (rows, LANES)

    def count_ge(thr):
        def body(ci, acc):
            for j in range(nt):
                k = keys_ref[ci, :, j * LANES:(j + 1) * LANES]
                acc = acc + jnp.where(k >= thr, 1.0, 0.0)
            return acc
        return _lane_sum(lax.fori_loop(0, nch, body, jnp.zeros(shape, F32)))

    def active(lo, hi):
        return (lo + 1) != hi

    def cond(state):
        it, lo, hi, _ = state
        return jnp.logical_and(it < MAX_BISECT_STEPS,
                               jnp.max(jnp.where(active(lo, hi), 1.0, 0.0)) > 0.0)

    def body(state):
        it, lo, hi, clo = state
        mid = (lo & hi) + lax.shift_right_arithmetic(lo ^ hi, 1)
        cnt = count_ge(mid)
        ge = cnt >= nsel
        exact = cnt == nsel
        act = active(lo, hi)
        new_hi = jnp.where(exact, mid + 1, jnp.where(ge, hi, mid))
        return (it + 1,
                jnp.where(jnp.logical_and(act, ge), mid, lo),
                jnp.where(act, new_hi, hi),
                jnp.where(jnp.logical_and(act, ge), cnt, clo))

    state = (jnp.int32(0), jnp.full(shape, INT_MIN + 1, I32), jnp.full(shape, INT_MAX, I32),
             jnp.zeros(shape, F32))
    _, t, _, clo = lax.while_loop(cond, body, state)

    @pl.when(jnp.max(clo) > nsel)
    def _():
        need = nsel - count_ge(t + 1)
        strict_upper = (lax.broadcasted_iota(I32, (LANES, LANES), 0)
                        < lax.broadcasted_iota(I32, (LANES, LANES), 1)).astype(BF16)

        def tie_body(ci, seen):
            for j in range(nt):
                k = keys_ref[ci, :, j * LANES:(j + 1) * LANES]
                eq = k == t
                eqf = jnp.where(eq, 1.0, 0.0)
                before = seen + jnp.dot(eqf.astype(BF16), strict_upper, preferred_element_type=F32)
                lose = jnp.logical_and(eq, before >= need)
                keys_ref[ci, :, j * LANES:(j + 1) * LANES] = jnp.where(lose, INT_MIN, k)
                seen = seen + _lane_sum(eqf)
            return seen

        lax.fori_loop(0, nch, tie_body, jnp.zeros(shape, F32))

    return t


def _dsa_prompt_kernel(aq_ref, qi_ref, wi_ref, kbf_ref, vbf_ref, kit_ref, o_ref,
                       keys_ref, wb_ref, m_ref, l_ref, acc_ref, *, nsel, kc):
    qb = Q_BLOCK
    nt = kc // LANES
    i = pl.program_id(0)
    nch = ((i + 1) * qb + kc - 1) // kc

    qi = qi_ref[...]
    qhm = jnp.concatenate([qi[:, h * D_IDX:(h + 1) * D_IDX] for h in range(H_IDX)], axis=0).astype(BF16)
    wi = wi_ref[...] * IDX_SCALE
    for h in range(H_IDX):
        wb_ref[h] = jnp.broadcast_to(wi[:, h:h + 1], (qb, LANES))
    qpos = i * qb + lax.broadcasted_iota(I32, (qb, LANES), 0)
    lane = lax.broadcasted_iota(I32, (qb, LANES), 1)

    def score_body(ci, carry):
        s = jnp.dot(qhm, kit_ref[ci], preferred_element_type=F32)
        for j in range(nt):
            acc = jnp.zeros((qb, LANES), F32)
            for h in range(H_IDX):
                acc = acc + jnp.maximum(s[h * qb:(h + 1) * qb, j * LANES:(j + 1) * LANES], 0.0) * wb_ref[h]
            kpos = ci * kc + j * LANES + lane
            keys_ref[ci, :, j * LANES:(j + 1) * LANES] = jnp.where(kpos <= qpos, _sortable_key(acc), INT_MIN)
        return carry

    lax.fori_loop(0, nch, score_body, 0)
    t = _topk_threshold(keys_ref, nch, nsel, qb, kc)

    for h in range(H_ATT):
        m_ref[h] = jnp.full((qb, LANES), NEG_BIG, F32)
        l_ref[h] = jnp.zeros((qb, LANES), F32)
        acc_ref[h] = jnp.zeros((qb, LANES), F32)
    aq = aq_ref[...]
    qh = [aq[:, h * D_HEAD:(h + 1) * D_HEAD].astype(BF16) for h in range(H_ATT)]

    def att_body(ci, carry):
        r0 = pl.multiple_of(ci * kc, kc)
        sel = [keys_ref[ci, :, j * LANES:(j + 1) * LANES] >= t for j in range(nt)]
        for h in range(H_ATT):
            hs = slice(h * D_HEAD, (h + 1) * D_HEAD)
            lg = lax.dot_general(qh[h], kbf_ref[pl.ds(r0, kc), hs], NT_DIMS,
                                 preferred_element_type=F32) * ATT_SCALE
            lgs = [lg[:, j * LANES:(j + 1) * LANES] for j in range(nt)]
            mx = functools.reduce(jnp.maximum, [jnp.where(sel[j], lgs[j], NEG_BIG) for j in range(nt)])
            m_old = m_ref[h]
            m_new = jnp.maximum(m_old, jnp.broadcast_to(jnp.max(mx, axis=1, keepdims=True), mx.shape))
            alpha = jnp.exp(m_old - m_new)
            p = [jnp.where(sel[j], jnp.exp(lgs[j] - m_new), 0.0) for j in range(nt)]
            l_ref[h] = alpha * l_ref[h] + _lane_sum(functools.reduce(jnp.add, p))
            pv = jnp.dot(jnp.concatenate(p, axis=1).astype(BF16), vbf_ref[pl.ds(r0, kc), hs],
                         preferred_element_type=F32)
            acc_ref[h] = alpha * acc_ref[h] + pv
            m_ref[h] = m_new
        return carry

    lax.fori_loop(0, nch, att_body, 0)
    for h in range(H_ATT):
        o_ref[:, h * D_HEAD:(h + 1) * D_HEAD] = acc_ref[h] / l_ref[h]


def _dsa_prompt(aq, qi, wi, kbf, vbf, kit, *, nsel):
    t = aq.shape[0]
    nchunks, _, kc = kit.shape
    qb = Q_BLOCK
    assert t % qb == 0 and nchunks * kc == t
    row = lambda w: pl.BlockSpec((qb, w), lambda i: (i, 0))
    state = pltpu.VMEM((H_ATT, qb, LANES), F32)
    vmem = kbf.size * 2 * 2 + kit.size * 2 + nchunks * qb * kc * 4 + (12 << 20)
    return pl.pallas_call(
        functools.partial(_dsa_prompt_kernel, nsel=nsel, kc=kc),
        grid=(t // qb,),
        in_specs=[row(SEG), row(SEG), row(H_IDX), _resident(kbf.shape), _resident(vbf.shape),
                  _resident(kit.shape)],
        out_specs=row(SEG),
        out_shape=jax.ShapeDtypeStruct((t, SEG), F32),
        scratch_shapes=[pltpu.VMEM((nchunks, qb, kc), I32), pltpu.VMEM((H_IDX, qb, LANES), F32),
                        state, state, state],
        compiler_params=_vmem_params(vmem, ("arbitrary",)),
        name="dsa_prompt",
    )(aq, qi, wi, kbf, vbf, kit)


def _pad_rows(a, n):
    return jnp.concatenate([a, jnp.zeros((n - a.shape[0], a.shape[1]), a.dtype)], axis=0)


def _sample_scores_kernel(pt_ref, kidx_ref, qi_ref, wi_ref, kin_ref, keys_ref, thr_ref, *, nsel, t_valid):
    del pt_ref
    p = pl.program_id(1)
    n_pages = pl.num_programs(1) - 1
    page = kidx_ref.shape[2]
    rows = SAMPLE_ROWS
    qi = qi_ref[0]
    qhm = jnp.concatenate([qi[:, h * D_IDX:(h + 1) * D_IDX] for h in range(H_IDX)], axis=0).astype(BF16)
    wi = wi_ref[0] * IDX_SCALE

    def store_keys(s, allowed):
        acc = jnp.zeros((rows, page), F32)
        for h in range(H_IDX):
            acc = acc + jnp.maximum(s[h * rows:(h + 1) * rows], 0.0) * wi[:, h:h + 1]
        key = _sortable_key(acc)
        keys_ref[0, p] = key if allowed is None else jnp.where(allowed, key, INT_MIN)

    @pl.when(p < n_pages)
    def _():
        store_keys(jnp.dot(qhm, kidx_ref[0].astype(BF16), preferred_element_type=F32), None)

    @pl.when(p == n_pages)
    def _():
        kn = _pad_rows(kin_ref[0], page).astype(BF16)
        r = lax.broadcasted_iota(I32, (rows, page), 0)
        c = lax.broadcasted_iota(I32, (rows, page), 1)
        store_keys(lax.dot_general(qhm, kn, NT_DIMS, preferred_element_type=F32),
                   jnp.logical_and(c <= r, c < t_valid))
        thr_ref[0] = _topk_threshold(keys_ref.at[0], n_pages + 1, nsel, rows, page)


def _sample_scores(page_table, kidx_t, qi, wi, ki_new, *, nsel, t_valid):
    b, n_pages = page_table.shape
    _, _, page = kidx_t.shape
    assert page == LANES
    rows = SAMPLE_ROWS
    page_idx = lambda i, p, pt: (pt[i, jnp.minimum(p, n_pages - 1)], 0, 0)
    grid_spec = pltpu.PrefetchScalarGridSpec(
        num_scalar_prefetch=1,
        grid=(b, n_pages + 1),
        in_specs=[pl.BlockSpec((1, D_IDX, page), page_idx),
                  pl.BlockSpec((1, rows, SEG), lambda i, p, pt: (i, 0, 0)),
                  pl.BlockSpec((1, rows, H_IDX), lambda i, p, pt: (i, 0, 0)),
                  pl.BlockSpec((1, rows, D_IDX), lambda i, p, pt: (i, 0, 0))],
        out_specs=[pl.BlockSpec((1, n_pages + 1, rows, page), lambda i, p, pt: (i, 0, 0, 0)),
                   pl.BlockSpec((1, rows, LANES), lambda i, p, pt: (i, 0, 0))],
    )
    return pl.pallas_call(
        functools.partial(_sample_scores_kernel, nsel=nsel, t_valid=t_valid),
        grid_spec=grid_spec,
        out_shape=[jax.ShapeDtypeStruct((b, n_pages + 1, rows, page), I32),
                   jax.ShapeDtypeStruct((b, rows, LANES), I32)],
        compiler_params=_vmem_params(32 << 20, ("parallel", "arbitrary")),
        name="sample_scores",
    )(page_table, kidx_t, qi, wi, ki_new)


def _sample_attn_kernel(pt_ref, kc_ref, vc_ref, aq_ref, kn_ref, vn_ref, keys_ref, thr_ref, o_ref,
                        m_ref, l_ref, acc_ref):
    del pt_ref
    p = pl.program_id(1)
    n_pages = pl.num_programs(1) - 1
    page = kc_ref.shape[1]
    rows = SAMPLE_ROWS

    @pl.when(p == 0)
    def _():
        for h in range(H_ATT):
            m_ref[h] = jnp.full((rows, LANES), NEG_BIG, F32)
            l_ref[h] = jnp.zeros((rows, LANES), F32)
            acc_ref[h] = jnp.zeros((rows, LANES), F32)

    sel = keys_ref[0, 0] >= thr_ref[0]
    aq = aq_ref[0]

    def attend(h, kp, vp):
        hs = slice(h * D_HEAD, (h + 1) * D_HEAD)
        lg = lax.dot_general(aq[:, hs].astype(BF16), kp, NT_DIMS, preferred_element_type=F32) * ATT_SCALE
        mx = jnp.max(jnp.where(sel, lg, NEG_BIG), axis=1, keepdims=True)
        m_old = m_ref[h]
        m_new = jnp.maximum(m_old, jnp.broadcast_to(mx, m_old.shape))
        alpha = jnp.exp(m_old - m_new)
        pr = jnp.where(sel, jnp.exp(lg - m_new), 0.0)
        l_ref[h] = alpha * l_ref[h] + _lane_sum(pr)
        acc_ref[h] = alpha * acc_ref[h] + jnp.dot(pr.astype(BF16), vp, preferred_element_type=F32)
        m_ref[h] = m_new

    @pl.when(p < n_pages)
    def _():
        for h in range(H_ATT):
            attend(h, kc_ref[0, :, h, :].astype(BF16), vc_ref[0, :, h, :].astype(BF16))

    @pl.when(p == n_pages)
    def _():
        for h in range(H_ATT):
            hs = slice(h * D_HEAD, (h + 1) * D_HEAD)
            attend(h, _pad_rows(kn_ref[0, :, hs], page).astype(BF16),
                   _pad_rows(vn_ref[0, :, hs], page).astype(BF16))
            o_ref[0, :, hs] = acc_ref[h] / l_ref[h]


def _sample_attn(page_table, cache_k, cache_v, aq, k_new, v_new, keys, thr):
    b, n_pages = page_table.shape
    _, page, _, _ = cache_k.shape
    rows = SAMPLE_ROWS
    page_idx = lambda i, p, pt: (pt[i, jnp.minimum(p, n_pages - 1)], 0, 0, 0)
    seq = lambda w: pl.BlockSpec((1, rows, w), lambda i, p, pt: (i, 0, 0))
    state = pltpu.VMEM((H_ATT, rows, LANES), F32)
    cache_page = pl.BlockSpec((1, page, H_ATT, D_HEAD), page_idx)
    grid_spec = pltpu.PrefetchScalarGridSpec(
        num_scalar_prefetch=1,
        grid=(b, n_pages + 1),
        in_specs=[cache_page, cache_page,
                  seq(SEG), seq(SEG), seq(SEG),
                  pl.BlockSpec((1, 1, rows, page), lambda i, p, pt: (i, p, 0, 0)),
                  seq(LANES)],
        out_specs=seq(SEG),
        scratch_shapes=[state, state, state],
    )
    return pl.pallas_call(
        _sample_attn_kernel,
        grid_spec=grid_spec,
        out_shape=jax.ShapeDtypeStruct((b, rows, SEG), F32),
        compiler_params=_vmem_params(32 << 20, ("parallel", "arbitrary")),
        name="sample_attn",
    )(page_table, cache_k, cache_v, aq, k_new, v_new, keys, thr)


def _merge_kernel(x_ref, oh_ref, gh_ref, oa_ref, ga_ref, gnh_ref, gna_ref, wo_ref, lng_ref, lnb_ref, y_ref):
    def branch(o_ref, gate_ref, gain_ref):
        parts = []
        for h in range(H_ATT):
            o = o_ref[:, h * D_HEAD:(h + 1) * D_HEAD]
            parts.append(o * lax.rsqrt(jnp.mean(o * o, axis=-1, keepdims=True) + EPS))
        g = gate_ref[...]
        return (jnp.concatenate(parts, axis=1) * gain_ref[...]) * (g * _sigmoid(g))

    hh = branch(oh_ref, gh_ref, gnh_ref).astype(BF16)
    aa = branch(oa_ref, ga_ref, gna_ref).astype(BF16)
    mix = (jnp.dot(hh, wo_ref[:SEG, :], preferred_element_type=F32)
           + jnp.dot(aa, wo_ref[SEG:, :], preferred_element_type=F32))
    r = ALPHA * x_ref[...] + mix
    mu = jnp.mean(r, axis=-1, keepdims=True)
    rc = r - mu
    var = jnp.mean(rc * rc, axis=-1, keepdims=True)
    y_ref[...] = rc * lax.rsqrt(var + EPS) * lng_ref[...] + lnb_ref[...]


def _merge(x2, oh, gh, oa, ga, gnh, gna, wo, lng, lnb, *, tm):
    t, d = x2.shape
    assert t % tm == 0
    row = lambda w: pl.BlockSpec((tm, w), lambda i: (i, 0))
    return pl.pallas_call(
        _merge_kernel,
        grid=(t // tm,),
        in_specs=[row(d), row(SEG), row(SEG), row(SEG), row(SEG), _resident(gnh.shape), _resident(gna.shape),
                  _resident(wo.shape), _resident(lng.shape), _resident(lnb.shape)],
        out_specs=row(d),
        out_shape=jax.ShapeDtypeStruct((t, d), F32),
        compiler_params=_vmem_params(40 << 20, ("parallel",)),
        name="merge",
    )(x2, oh, gh, oa, ga, gnh, gna, wo, lng, lnb)


def kernel(x_prompt, x_sample, cache_k, cache_v, cache_kidx, state_hgrn, page_table,
           w_in, w_out, lb_logits, g_hgrn, g_attn, ln_g, ln_b):
    bp, t, d = x_prompt.shape
    bs, ts, _ = x_sample.shape
    n_pages = page_table.shape[1]
    page = cache_k.shape[2]
    assert w_in.shape[0] == DEPTH and bp == 1 and ts <= SAMPLE_ROWS
    main = N_SEG * SEG

    w = w_in[0]
    wm = w[:, :main].astype(BF16)
    wt = jnp.pad(w[:, main:], ((0, 0), (0, LANES - (D_IDX + H_IDX)))).astype(BF16)
    wikt = w[:, main:main + D_IDX].T.astype(BF16)
    wo = w_out[0].astype(BF16)
    gnh, gna = g_hgrn[0][None, :], g_attn[0][None, :]
    lng, lnb = ln_g[0][None, :], ln_b[0][None, :]

    xp = x_prompt.reshape(t, d)
    kc = 2 * LANES
    (q, kk, v, lf, gh, aq, ak, av, ag, iq, ik, iw, kbf, vbf, kit) = _project(
        xp, wm, wt, wikt, lb_logits, tm=256, kc=kc)
    rs = lambda a: a.reshape(1, t, SEG)
    s0 = jnp.zeros((1, H_HGRN, D_HEAD, D_HEAD), F32)
    o_h, s_p = _hgrn(rs(q), rs(kk), rs(v), rs(lf), s0, rows=256, t_valid=HGRN_CHUNK)
    o_a = _dsa_prompt(aq, iq, iw, kbf, vbf, kit, nsel=min(TOPK_MAX, t // 4))
    y_p = _merge(xp, o_h.reshape(t, SEG), gh, o_a, ag, gnh, gna, wo, lng, lnb, tm=512)

    rows = SAMPLE_ROWS
    xs = jnp.pad(x_sample, ((0, 0), (0, rows - ts), (0, 0))).reshape(bs * rows, d)
    (q, kk, v, lf, gh, aq, ak_s, av_s, ag, iq, ik_s, iw, _, _) = _project(
        xs, wm, wt, wikt, lb_logits, tm=bs * rows, kc=0)
    rs = lambda a: a.reshape(bs, rows, a.shape[-1])
    o_h, s_s = _hgrn(rs(q), rs(kk), rs(v), rs(lf), state_hgrn[0], rows=rows, t_valid=ts)
    nsel = min(TOPK_MAX, (n_pages * page + ts) // 4)
    kidx_t = jnp.swapaxes(cache_kidx[0], 1, 2)
    keys, thr = _sample_scores(page_table, kidx_t, rs(iq), rs(iw), rs(ik_s), nsel=nsel, t_valid=ts)
    o_a = _sample_attn(page_table, cache_k[0], cache_v[0], rs(aq), rs(ak_s), rs(av_s), keys, thr)
    y_s = _merge(xs, o_h.reshape(bs * rows, SEG), gh, o_a.reshape(bs * rows, SEG), ag,
                 gnh, gna, wo, lng, lnb, tm=bs * rows)

    heads = lambda a, n: a.reshape(1, 1, n, H_ATT, D_HEAD)
    sample = lambda a: a.reshape(bs, rows, -1)[:, :ts]
    return (y_p.reshape(1, t, d),
            sample(y_s),
            heads(ak, t), heads(av, t), ik.reshape(1, 1, t, D_IDX), s_p[None],
            sample(ak_s).reshape(1, bs, ts, H_ATT, D_HEAD), sample(av_s).reshape(1, bs, ts, H_ATT, D_HEAD),
            sample(ik_s)[None], s_s[None])
```

```python
import functools

import jax
import jax.numpy as jnp
from jax import lax
from jax.experimental import pallas as pl
from jax.experimental.pallas import tpu as pltpu

F32 = jnp.float32
BF16 = jnp.bfloat16
I32 = jnp.int32

H_HGRN = 4
H_ATT = 4
D_HEAD = 128
H_IDX = 8
D_IDX = 64
SEG = 512
N_SEG = 9
TOPK_MAX = 256
HGRN_CHUNK = 64
Q_BLOCK = 128
EPS = 1e-5
DEPTH = 1
ALPHA = (2 * DEPTH) ** 0.25
IDX_SCALE = H_IDX ** -0.5 * D_IDX ** -0.5
ATT_SCALE = D_HEAD ** -0.5

LANES = 128
SUBLANES = 8
SAMPLE_ROWS = SUBLANES

INT_MIN = -(2 ** 31)
INT_MAX = 2 ** 31 - 1
NEG_BIG = -1e30
MAX_BISECT_STEPS = 34

NT_DIMS = (((1,), (1,)), ((), ()))
TN_DIMS = (((0,), (0,)), ((), ()))


def _sigmoid(x):
    return 1.0 / (1.0 + jnp.exp(-x))


def _vmem_params(nbytes, semantics):
    return pltpu.CompilerParams(dimension_semantics=semantics, vmem_limit_bytes=int(nbytes))


def _resident(shape):
    zeros = (0,) * len(shape)
    return pl.BlockSpec(shape, lambda *_: zeros, pipeline_mode=pl.Buffered(1))


def _proj_kernel(x_ref, wm_ref, wt_ref, wikt_ref, lbl_ref,
                 q_ref, kk_ref, v_ref, lf_ref, gh_ref, aq_ref, ak_ref, av_ref, ag_ref, iq_ref,
                 ik_ref, iw_ref, kbf_ref, vbf_ref, *maybe_kit_ref, kc):
    x = x_ref[...].astype(BF16)

    def seg(j):
        return jnp.dot(x, wm_ref[:, j * SEG:(j + 1) * SEG], preferred_element_type=F32)

    hq = seg(0)
    q_ref[...] = hq * _sigmoid(hq)
    lbl = lbl_ref[...]
    e = jnp.exp(lbl - jnp.max(lbl, axis=0, keepdims=True))
    lb = e[0:1] / jnp.sum(e, axis=0, keepdims=True)
    f = lb + (1.0 - lb) * _sigmoid(seg(1))
    kk_ref[...] = 1.0 - f
    lf_ref[...] = jnp.log(f)
    v_ref[...] = seg(2)
    gh_ref[...] = seg(3)
    aq_ref[...] = seg(4)
    ak = seg(5)
    ak_ref[...] = ak
    kbf_ref[...] = ak.astype(BF16)
    av = seg(6)
    av_ref[...] = av
    vbf_ref[...] = av.astype(BF16)
    ag_ref[...] = seg(7)
    iq_ref[...] = seg(8)
    tail = jnp.dot(x, wt_ref[...], preferred_element_type=F32)
    ik_ref[...] = tail[:, :D_IDX]
    iw_ref[...] = tail[:, D_IDX:D_IDX + H_IDX]
    if maybe_kit_ref:
        (kit_ref,) = maybe_kit_ref
        kt = lax.dot_general(wikt_ref[...], x, NT_DIMS, preferred_element_type=F32).astype(BF16)
        for j in range(kt.shape[1] // kc):
            kit_ref[j] = kt[:, j * kc:(j + 1) * kc]


def _project(x2, wm, wt, wikt, lbl, *, tm, kc):
    t, d = x2.shape
    assert t % tm == 0 and (kc == 0 or tm % kc == 0)
    row = lambda w: pl.BlockSpec((tm, w), lambda i: (i, 0))
    out_shape = [jax.ShapeDtypeStruct((t, SEG), F32)] * 10 + [
        jax.ShapeDtypeStruct((t, D_IDX), F32), jax.ShapeDtypeStruct((t, H_IDX), F32),
        jax.ShapeDtypeStruct((t, SEG), BF16), jax.ShapeDtypeStruct((t, SEG), BF16)]
    out_specs = [row(SEG)] * 10 + [row(D_IDX), row(H_IDX), row(SEG), row(SEG)]
    if kc:
        out_shape.append(jax.ShapeDtypeStruct((t // kc, D_IDX, kc), BF16))
        out_specs.append(pl.BlockSpec((tm // kc, D_IDX, kc), lambda i: (i, 0, 0)))
    vmem = wm.size * 2 + 2 * tm * d * 4 + 2 * tm * SEG * (10 * 4 + 2 * 2) + (8 << 20)
    return pl.pallas_call(
        functools.partial(_proj_kernel, kc=kc),
        grid=(t // tm,),
        in_specs=[row(d), _resident(wm.shape), _resident(wt.shape), _resident(wikt.shape),
                  _resident(lbl.shape)],
        out_specs=out_specs,
        out_shape=out_shape,
        compiler_params=_vmem_params(vmem, ("parallel",)),
        name="proj",
    )(x2, wm, wt, wikt, lbl)


def _hgrn_kernel(q_ref, kk_ref, v_ref, lf_ref, s0_ref, o_ref, sout_ref, st_ref, *, rows, t_valid):
    c_len = HGRN_CHUNK
    step = pl.program_id(1)

    @pl.when(step == 0)
    def _():
        for h in range(H_HGRN):
            st_ref[h] = s0_ref[0, h].T

    r = lax.broadcasted_iota(I32, (c_len, c_len), 0)
    c = lax.broadcasted_iota(I32, (c_len, c_len), 1)
    causal = r >= c
    tri = causal.astype(F32)
    pad = c_len - rows if rows < c_len else 0

    def load(ref, r0, n, h):
        a = ref[0, r0:r0 + n, h * D_HEAD:(h + 1) * D_HEAD]
        if pad:
            a = jnp.concatenate([a, jnp.zeros((pad, D_HEAD), F32)], axis=0)
        return a

    n_rows = min(rows, c_len)
    for ci in range(max(rows // c_len, 1)):
        r0 = ci * c_len
        for h in range(H_HGRN):
            q = load(q_ref, r0, n_rows, h)
            kk = load(kk_ref, r0, n_rows, h)
            v = load(v_ref, r0, n_rows, h)
            lf = load(lf_ref, r0, n_rows, h)
            if t_valid < c_len:
                lf = jnp.where(lax.broadcasted_iota(I32, lf.shape, 0) < t_valid, lf, 0.0)
            b = jnp.dot(tri, lf, precision=lax.Precision.HIGHEST, preferred_element_type=F32)
            b_end = b[c_len - 1:c_len, :]
            qd = (q * jnp.exp(b)).astype(BF16)
            kd = (kk * jnp.exp(-b)).astype(BF16)
            k2 = (kk * jnp.exp(b_end - b)).astype(BF16)
            vb = v.astype(BF16)
            att = lax.dot_general(qd, kd, NT_DIMS, preferred_element_type=F32)
            att = jnp.where(causal, att, 0.0).astype(BF16)
            st = st_ref[h]
            o = lax.dot_general(qd, st.astype(BF16), NT_DIMS, preferred_element_type=F32)
            o = o + jnp.dot(att, vb, preferred_element_type=F32)
            st_ref[h] = st * jnp.exp(b_end) + lax.dot_general(vb, k2, TN_DIMS, preferred_element_type=F32)
            o_ref[0, r0:r0 + n_rows, h * D_HEAD:(h + 1) * D_HEAD] = o[:n_rows]

    @pl.when(step == pl.num_programs(1) - 1)
    def _():
        for h in range(H_HGRN):
            sout_ref[0, h] = st_ref[h].T


def _hgrn(q, kk, v, lf, s0, *, rows, t_valid):
    b, t, _ = q.shape
    assert t % rows == 0
    blk = pl.BlockSpec((1, rows, SEG), lambda i, j: (i, j, 0))
    sblk = pl.BlockSpec((1, H_HGRN, D_HEAD, D_HEAD), lambda i, j: (i, 0, 0, 0))
    return pl.pallas_call(
        functools.partial(_hgrn_kernel, rows=rows, t_valid=t_valid),
        grid=(b, t // rows),
        in_specs=[blk, blk, blk, blk, sblk],
        out_specs=[blk, sblk],
        out_shape=[jax.ShapeDtypeStruct((b, t, SEG), F32),
                   jax.ShapeDtypeStruct((b, H_HGRN, D_HEAD, D_HEAD), F32)],
        scratch_shapes=[pltpu.VMEM((H_HGRN, D_HEAD, D_HEAD), F32)],
        compiler_params=_vmem_params(32 << 20, ("parallel", "arbitrary")),
        name="hgrn",
    )(q, kk, v, lf, s0)


def _sortable_key(x):
    b = lax.bitcast_convert_type(jnp.where(x == 0.0, 0.0, x), I32)
    return b ^ (lax.shift_right_arithmetic(b, 31) & INT_MAX)


def _lane_sum(x):
    return jnp.broadcast_to(jnp.sum(x, axis=1, keepdims=True), x.shape)


def _topk_threshold(keys_ref, nch, nsel, rows, kc):
    nt = kc // LANES
    shape = (rows, LANES)

    def count_ge(thr):
        def body(ci, acc):
            for j in range(nt):
                k = keys_ref[ci, :, j * LANES:(j + 1) * LANES]
                acc = acc + jnp.where(k >= thr, 1.0, 0.0)
            return acc
        return _lane_sum(lax.fori_loop(0, nch, body, jnp.zeros(shape, F32)))

    def active(lo, hi):
        return (lo + 1) != hi

    def cond(state):
        it, lo, hi, _ = state
        return jnp.logical_and(it < MAX_BISECT_STEPS,
                               jnp.max(jnp.where(active(lo, hi), 1.0, 0.0)) > 0.0)

    def body(state):
        it, lo, hi, clo = state
        mid = (lo & hi) + lax.shift_right_arithmetic(lo ^ hi, 1)
        cnt = count_ge(mid)
        ge = cnt >= nsel
        exact = cnt == nsel
        act = active(lo, hi)
        new_hi = jnp.where(exact, mid + 1, jnp.where(ge, hi, mid))
        return (it + 1,
                jnp.where(jnp.logical_and(act, ge), mid, lo),
                jnp.where(act, new_hi, hi),
                jnp.where(jnp.logical_and(act, ge), cnt, clo))

    state = (jnp.int32(0), jnp.full(shape, INT_MIN + 1, I32), jnp.full(shape, INT_MAX, I32),
             jnp.zeros(shape, F32))
    _, t, _, clo = lax.while_loop(cond, body, state)

    @pl.when(jnp.max(clo) > nsel)
    def _():
        need = nsel - count_ge(t + 1)
        strict_upper = (lax.broadcasted_iota(I32, (LANES, LANES), 0)
                        < lax.broadcasted_iota(I32, (LANES, LANES), 1)).astype(BF16)

        def tie_body(ci, seen):
            for j in range(nt):
                k = keys_ref[ci, :, j * LANES:(j + 1) * LANES]
                eq = k == t
                eqf = jnp.where(eq, 1.0, 0.0)
                before = seen + jnp.dot(eqf.astype(BF16), strict_upper, preferred_element_type=F32)
                lose = jnp.logical_and(eq, before >= need)
                keys_ref[ci, :, j * LANES:(j + 1) * LANES] = jnp.where(lose, INT_MIN, k)
                seen = seen + _lane_sum(eqf)
            return seen

        lax.fori_loop(0, nch, tie_body, jnp.zeros(shape, F32))

    return t


def _dsa_prompt_kernel(aq_ref, qi_ref, wi_ref, kbf_ref, vbf_ref, kit_ref, o_ref,
                       keys_ref, wb_ref, m_ref, l_ref, acc_ref, *, nsel, kc):
    qb = Q_BLOCK
    nt = kc // LANES
    i = pl.program_id(0)
    nch = ((i + 1) * qb + kc - 1) // kc

    qi = qi_ref[...]
    qhm = jnp.concatenate([qi[:, h * D_IDX:(h + 1) * D_IDX] for h in range(H_IDX)], axis=0).astype(BF16)
    wi = wi_ref[...] * IDX_SCALE
    for h in range(H_IDX):
        wb_ref[h] = jnp.broadcast_to(wi[:, h:h + 1], (qb, LANES))
    qpos = i * qb + lax.broadcasted_iota(I32, (qb, LANES), 0)
    lane = lax.broadcasted_iota(I32, (qb, LANES), 1)

    def score_body(ci, carry):
        s = jnp.dot(qhm, kit_ref[ci], preferred_element_type=F32)
        for j in range(nt):
            acc = jnp.zeros((qb, LANES), F32)
            for h in range(H_IDX):
                acc = acc + jnp.maximum(s[h * qb:(h + 1) * qb, j * LANES:(j + 1) * LANES], 0.0) * wb_ref[h]
            kpos = ci * kc + j * LANES + lane
            keys_ref[ci, :, j * LANES:(j + 1) * LANES] = jnp.where(kpos <= qpos, _sortable_key(acc), INT_MIN)
        return carry

    lax.fori_loop(0, nch, score_body, 0)
    t = _topk_threshold(keys_ref, nch, nsel, qb, kc)

    for h in range(H_ATT):
        m_ref[h] = jnp.full((qb, LANES), NEG_BIG, F32)
        l_ref[h] = jnp.zeros((qb, LANES), F32)
        acc_ref[h] = jnp.zeros((qb, LANES), F32)
    aq = aq_ref[...]
    qh = [aq[:, h * D_HEAD:(h + 1) * D_HEAD].astype(BF16) for h in range(H_ATT)]

    def att_body(ci, carry):
        r0 = pl.multiple_of(ci * kc, kc)
        sel = [keys_ref[ci, :, j * LANES:(j + 1) * LANES] >= t for j in range(nt)]
        for h in range(H_ATT):
            hs = slice(h * D_HEAD, (h + 1) * D_HEAD)
            lg = lax.dot_general(qh[h], kbf_ref[pl.ds(r0, kc), hs], NT_DIMS,
                                 preferred_element_type=F32) * ATT_SCALE
            lgs = [lg[:, j * LANES:(j + 1) * LANES] for j in range(nt)]
            mx = functools.reduce(jnp.maximum, [jnp.where(sel[j], lgs[j], NEG_BIG) for j in range(nt)])
            m_old = m_ref[h]
            m_new = jnp.maximum(m_old, jnp.broadcast_to(jnp.max(mx, axis=1, keepdims=True), mx.shape))
            alpha = jnp.exp(m_old - m_new)
            p = [jnp.where(sel[j], jnp.exp(lgs[j] - m_new), 0.0) for j in range(nt)]
            l_ref[h] = alpha * l_ref[h] + _lane_sum(functools.reduce(jnp.add, p))
            pv = jnp.dot(jnp.concatenate(p, axis=1).astype(BF16), vbf_ref[pl.ds(r0, kc), hs],
                         preferred_element_type=F32)
            acc_ref[h] = alpha * acc_ref[h] + pv
            m_ref[h] = m_new
        return carry

    lax.fori_loop(0, nch, att_body, 0)
    for h in range(H_ATT):
        o_ref[:, h * D_HEAD:(h + 1) * D_HEAD] = acc_ref[h] / l_ref[h]


def _dsa_prompt(aq, qi, wi, kbf, vbf, kit, *, nsel):
    t = aq.shape[0]
    nchunks, _, kc = kit.shape
    qb = Q_BLOCK
    assert t % qb == 0 and nchunks * kc == t
    row = lambda w: pl.BlockSpec((qb, w), lambda i: (i, 0))
    state = pltpu.VMEM((H_ATT, qb, LANES), F32)
    vmem = kbf.size * 2 * 2 + kit.size * 2 + nchunks * qb * kc * 4 + (12 << 20)
    return pl.pallas_call(
        functools.partial(_dsa_prompt_kernel, nsel=nsel, kc=kc),
        grid=(t // qb,),
        in_specs=[row(SEG), row(SEG), row(H_IDX), _resident(kbf.shape), _resident(vbf.shape),
                  _resident(kit.shape)],
        out_specs=row(SEG),
        out_shape=jax.ShapeDtypeStruct((t, SEG), F32),
        scratch_shapes=[pltpu.VMEM((nchunks, qb, kc), I32), pltpu.VMEM((H_IDX, qb, LANES), F32),
                        state, state, state],
        compiler_params=_vmem_params(vmem, ("arbitrary",)),
        name="dsa_prompt",
    )(aq, qi, wi, kbf, vbf, kit)


KEY_NEG_INF = -2139095041
KEY_POS_INF = 2139095040
SCORE_PAGES_PER_GROUP = 32
SCORE_PAGES_PER_DOT = 8
ATTN_PAGES_PER_GROUP = 8


def _pad_rows(a, n):
    return jnp.concatenate([a, jnp.zeros((n - a.shape[0], a.shape[1]), a.dtype)], axis=0)


def _key_to_float(k):
    return lax.bitcast_convert_type(k ^ (lax.shift_right_arithmetic(k, 31) & INT_MAX), F32)


def _topk_threshold_rows(sc_ref, nch, nsel, rows, rows_valid):
    shape = (rows, LANES)

    def count_ge(thr):
        parts = [jnp.zeros(shape, F32)] * 4
        for ci in range(nch):
            parts[ci % 4] = parts[ci % 4] + jnp.where(sc_ref[ci] >= thr, 1.0, 0.0)
        return _lane_sum((parts[0] + parts[1]) + (parts[2] + parts[3]))

    def active(lo, hi):
        return (lo + 1) != hi

    def cond(state):
        it, lo, hi, _ = state
        return jnp.logical_and(it < MAX_BISECT_STEPS,
                               jnp.max(jnp.where(active(lo, hi), 1.0, 0.0)) > 0.0)

    def body(state):
        it, lo, hi, clo = state
        mid = (lo & hi) + lax.shift_right_arithmetic(lo ^ hi, 1)
        cnt = count_ge(_key_to_float(mid))
        ge = cnt >= nsel
        exact = cnt == nsel
        act = active(lo, hi)
        new_hi = jnp.where(exact, mid + 1, jnp.where(ge, hi, mid))
        return (it + 1,
                jnp.where(jnp.logical_and(act, ge), mid, lo),
                jnp.where(act, new_hi, hi),
                jnp.where(jnp.logical_and(act, ge), cnt, clo))

    real = lax.broadcasted_iota(I32, shape, 0) < rows_valid
    state = (jnp.int32(0), jnp.full(shape, KEY_NEG_INF, I32),
             jnp.where(real, KEY_POS_INF + 1, KEY_NEG_INF + 1), jnp.zeros(shape, F32))
    _, lo, _, clo = lax.while_loop(cond, body, state)
    t = _key_to_float(lo)

    @pl.when(jnp.max(clo) > nsel)
    def _():
        need = nsel - count_ge(_key_to_float(lo + 1))
        strict_upper = (lax.broadcasted_iota(I32, (LANES, LANES), 0)
                        < lax.broadcasted_iota(I32, (LANES, LANES), 1)).astype(BF16)

        def tie_body(ci, seen):
            s = sc_ref[ci]
            eq = s == t
            eqf = jnp.where(eq, 1.0, 0.0)
            before = seen + jnp.dot(eqf.astype(BF16), strict_upper, preferred_element_type=F32)
            sc_ref[ci] = jnp.where(jnp.logical_and(eq, before >= need), jnp.nan, s)
            return seen + _lane_sum(eqf)

        lax.fori_loop(0, nch, tie_body, jnp.zeros(shape, F32))

    return t


def _page_group_copies(hbm_refs, bufs, sems, pt_ref, seq, group, slot, pages_per_group):
    copies = []
    for hbm, buf, sem in zip(hbm_refs, bufs, sems):
        for j in range(pages_per_group):
            page = pt_ref[seq, group * pages_per_group + j]
            copies.append(pltpu.make_async_copy(hbm.at[page], buf.at[slot, j], sem.at[slot]))
    return copies


def _stream_page_groups(hbm_refs, bufs, sems, pt_ref, n_groups, pages_per_group, consume, carry):
    assert n_groups % 2 == 0
    b = pl.program_id(0)
    n_seq = pl.num_programs(0)
    copies = functools.partial(_page_group_copies, hbm_refs, bufs, sems, pt_ref,
                               pages_per_group=pages_per_group)

    @pl.when(b == 0)
    def _():
        for c in copies(0, 0, 0):
            c.start()

    def pair_body(i, carry):
        for slot in range(2):
            g = 2 * i + slot

            @pl.when(g + 1 < n_groups)
            def _():
                for c in copies(b, g + 1, 1 - slot):
                    c.start()

            @pl.when(jnp.logical_and(g + 1 == n_groups, b + 1 < n_seq))
            def _():
                for c in copies(b + 1, 0, 1 - slot):
                    c.start()

            for c in copies(b, g, slot):
                c.wait()
            carry = consume(g, slot, carry)
        return carry

    return lax.fori_loop(0, n_groups // 2, pair_body, carry)


def _sample_scores_kernel(pt_ref, kidx_hbm, qi_ref, wi_ref, kin_ref, sc_ref, thr_ref, kbuf, sem,
                          *, n_pages, nsel, t_valid):
    rows = SAMPLE_ROWS
    gp, dp = SCORE_PAGES_PER_GROUP, SCORE_PAGES_PER_DOT
    qi = qi_ref[0]
    qhm = jnp.concatenate([qi[:, h * D_IDX:(h + 1) * D_IDX] for h in range(H_IDX)], axis=0).astype(BF16)
    wi = wi_ref[0] * IDX_SCALE

    def head_sum(s):
        acc = jnp.zeros((rows, s.shape[1]), F32)
        for h in range(H_IDX):
            acc = acc + jnp.maximum(s[h * rows:(h + 1) * rows], 0.0) * wi[:, h:h + 1]
        return acc

    def consume(g, slot, carry):
        for d in range(gp // dp):
            kt = jnp.concatenate([kbuf[slot, d * dp + j] for j in range(dp)], axis=1).astype(BF16)
            acc = head_sum(jnp.dot(qhm, kt, preferred_element_type=F32))
            for j in range(dp):
                sc_ref[0, g * gp + d * dp + j] = acc[:, j * LANES:(j + 1) * LANES]
        return carry

    _stream_page_groups([kidx_hbm], [kbuf], [sem], pt_ref, n_pages // gp, gp, consume, 0)

    kn = _pad_rows(kin_ref[0], LANES).astype(BF16)
    r = lax.broadcasted_iota(I32, (rows, LANES), 0)
    c = lax.broadcasted_iota(I32, (rows, LANES), 1)
    acc = head_sum(lax.dot_general(qhm, kn, NT_DIMS, preferred_element_type=F32))
    sc_ref[0, n_pages] = jnp.where(jnp.logical_and(c <= r, c < t_valid), acc, jnp.nan)
    thr_ref[0] = _topk_threshold_rows(sc_ref.at[0], n_pages + 1, nsel, rows, t_valid)


def _sample_scores(page_table, kidx_t, qi, wi, ki_new, *, nsel, t_valid):
    b, n_pages = page_table.shape
    _, _, page = kidx_t.shape
    gp = SCORE_PAGES_PER_GROUP
    assert page == LANES and n_pages % (2 * gp) == 0
    rows = SAMPLE_ROWS
    seq = lambda *tail: pl.BlockSpec((1,) + tail, lambda i, pt: (i,) + (0,) * len(tail))
    grid_spec = pltpu.PrefetchScalarGridSpec(
        num_scalar_prefetch=1,
        grid=(b,),
        in_specs=[pl.BlockSpec(memory_space=pl.ANY), seq(rows, SEG), seq(rows, H_IDX), seq(rows, D_IDX)],
        out_specs=[seq(n_pages + 1, rows, page), seq(rows, LANES)],
        scratch_shapes=[pltpu.VMEM((2, gp, D_IDX, page), F32), pltpu.SemaphoreType.DMA((2,))],
    )
    return pl.pallas_call(
        functools.partial(_sample_scores_kernel, n_pages=n_pages, nsel=nsel, t_valid=t_valid),
        grid_spec=grid_spec,
        out_shape=[jax.ShapeDtypeStruct((b, n_pages + 1, rows, page), F32),
                   jax.ShapeDtypeStruct((b, rows, LANES), F32)],
        compiler_params=_vmem_params(32 << 20, ("arbitrary",)),
        name="sample_scores",
    )(page_table, kidx_t, qi, wi, ki_new)


def _sample_attn_kernel(pt_ref, k_hbm, v_hbm, aq_ref, kn_ref, vn_ref, sc_ref, thr_ref, o_ref,
                        kbuf, vbuf, ksem, vsem, *, n_pages):
    rows = SAMPLE_ROWS
    gp = ATTN_PAGES_PER_GROUP
    page = LANES
    thr = thr_ref[0]
    aq = aq_ref[0]
    qh = [aq[:, h * D_HEAD:(h + 1) * D_HEAD].astype(BF16) for h in range(H_ATT)]

    def head_rows(buf, slot, j, h):
        return buf[slot, j, pl.ds(h, page, stride=H_ATT), :].astype(BF16)

    def attend(carry, sel, keys, values):
        m, l, acc = carry
        n = len(sel)
        m2, l2, acc2 = [], [], []
        for h in range(H_ATT):
            lg = [jnp.where(sel[j], lax.dot_general(qh[h], keys[j][h], NT_DIMS,
                                                    preferred_element_type=F32) * ATT_SCALE, NEG_BIG)
                  for j in range(n)]
            mx = functools.reduce(jnp.maximum, lg)
            m_new = jnp.maximum(m[h], jnp.broadcast_to(jnp.max(mx, axis=1, keepdims=True), mx.shape))
            alpha = jnp.exp(m[h] - m_new)
            p = [jnp.where(sel[j], jnp.exp(lg[j] - m_new), 0.0) for j in range(n)]
            pv = functools.reduce(jnp.add, [jnp.dot(p[j].astype(BF16), values[j][h],
                                                    preferred_element_type=F32) for j in range(n)])
            m2.append(m_new)
            l2.append(alpha * l[h] + _lane_sum(functools.reduce(jnp.add, p)))
            acc2.append(alpha * acc[h] + pv)
        return m2, l2, acc2

    def consume(g, slot, carry):
        sel = [sc_ref[0, g * gp + j] >= thr for j in range(gp)]
        keys = [[head_rows(kbuf, slot, j, h) for h in range(H_ATT)] for j in range(gp)]
        values = [[head_rows(vbuf, slot, j, h) for h in range(H_ATT)] for j in range(gp)]
        return attend(carry, sel, keys, values)

    init = ([jnp.full((rows, LANES), NEG_BIG, F32)] * H_ATT, [jnp.zeros((rows, LANES), F32)] * H_ATT,
            [jnp.zeros((rows, LANES), F32)] * H_ATT)
    carry = _stream_page_groups([k_hbm, v_hbm], [kbuf, vbuf], [ksem, vsem], pt_ref, n_pages // gp, gp,
                                consume, init)
    kn, vn = kn_ref[0], vn_ref[0]
    hs = lambda h: slice(h * D_HEAD, (h + 1) * D_HEAD)
    _, l, acc = attend(carry, [sc_ref[0, n_pages] >= thr],
                       [[_pad_rows(kn[:, hs(h)], page).astype(BF16) for h in range(H_ATT)]],
                       [[_pad_rows(vn[:, hs(h)], page).astype(BF16) for h in range(H_ATT)]])
    for h in range(H_ATT):
        o_ref[0, :, hs(h)] = acc[h] / l[h]


def _sample_attn(page_table, cache_k, cache_v, aq, k_new, v_new, scores, thr):
    b, n_pages = page_table.shape
    gp = ATTN_PAGES_PER_GROUP
    rows = SAMPLE_ROWS
    page_rows = cache_k.shape[1]
    assert page_rows == LANES * H_ATT and n_pages % (2 * gp) == 0
    seq = lambda *tail: pl.BlockSpec((1,) + tail, lambda i, pt: (i,) + (0,) * len(tail))
    buf = pltpu.VMEM((2, gp, page_rows, D_HEAD), F32)
    grid_spec = pltpu.PrefetchScalarGridSpec(
        num_scalar_prefetch=1,
        grid=(b,),
        in_specs=[pl.BlockSpec(memory_space=pl.ANY), pl.BlockSpec(memory_space=pl.ANY),
                  seq(rows, SEG), seq(rows, SEG), seq(rows, SEG), seq(n_pages + 1, rows, LANES), seq(rows, LANES)],
        out_specs=seq(rows, SEG),
        scratch_shapes=[buf, buf, pltpu.SemaphoreType.DMA((2,)), pltpu.SemaphoreType.DMA((2,))],
    )
    vmem = 2 * 2 * gp * page_rows * D_HEAD * 4 + (16 << 20)
    return pl.pallas_call(
        functools.partial(_sample_attn_kernel, n_pages=n_pages),
        grid_spec=grid_spec,
        out_shape=jax.ShapeDtypeStruct((b, rows, SEG), F32),
        compiler_params=_vmem_params(vmem, ("arbitrary",)),
        name="sample_attn",
    )(page_table, cache_k, cache_v, aq, k_new, v_new, scores, thr)


def _merge_kernel(x_ref, oh_ref, gh_ref, oa_ref, ga_ref, gnh_ref, gna_ref, wo_ref, lng_ref, lnb_ref, y_ref):
    def branch(o_ref, gate_ref, gain_ref):
        parts = []
        for h in range(H_ATT):
            o = o_ref[:, h * D_HEAD:(h + 1) * D_HEAD]
            parts.append(o * lax.rsqrt(jnp.mean(o * o, axis=-1, keepdims=True) + EPS))
        g = gate_ref[...]
        return (jnp.concatenate(parts, axis=1) * gain_ref[...]) * (g * _sigmoid(g))

    hh = branch(oh_ref, gh_ref, gnh_ref).astype(BF16)
    aa = branch(oa_ref, ga_ref, gna_ref).astype(BF16)
    mix = (jnp.dot(hh, wo_ref[:SEG, :], preferred_element_type=F32)
           + jnp.dot(aa, wo_ref[SEG:, :], preferred_element_type=F32))
    r = ALPHA * x_ref[...] + mix
    mu = jnp.mean(r, axis=-1, keepdims=True)
    rc = r - mu
    var = jnp.mean(rc * rc, axis=-1, keepdims=True)
    y_ref[...] = rc * lax.rsqrt(var + EPS) * lng_ref[...] + lnb_ref[...]


def _merge(x2, oh, gh, oa, ga, gnh, gna, wo, lng, lnb, *, tm):
    t, d = x2.shape
    assert t % tm == 0
    row = lambda w: pl.BlockSpec((tm, w), lambda i: (i, 0))
    return pl.pallas_call(
        _merge_kernel,
        grid=(t // tm,),
        in_specs=[row(d), row(SEG), row(SEG), row(SEG), row(SEG), _resident(gnh.shape), _resident(gna.shape),
                  _resident(wo.shape), _resident(lng.shape), _resident(lnb.shape)],
        out_specs=row(d),
        out_shape=jax.ShapeDtypeStruct((t, d), F32),
        compiler_params=_vmem_params(40 << 20, ("parallel",)),
        name="merge",
    )(x2, oh, gh, oa, ga, gnh, gna, wo, lng, lnb)


def kernel(x_prompt, x_sample, cache_k, cache_v, cache_kidx, state_hgrn, page_table,
           w_in, w_out, lb_logits, g_hgrn, g_attn, ln_g, ln_b):
    bp, t, d = x_prompt.shape
    bs, ts, _ = x_sample.shape
    n_pages = page_table.shape[1]
    page = cache_k.shape[2]
    assert w_in.shape[0] == DEPTH and bp == 1 and ts <= SAMPLE_ROWS
    main = N_SEG * SEG

    w = w_in[0]
    wm = w[:, :main].astype(BF16)
    wt = jnp.pad(w[:, main:], ((0, 0), (0, LANES - (D_IDX + H_IDX)))).astype(BF16)
    wikt = w[:, main:main + D_IDX].T.astype(BF16)
    wo = w_out[0].astype(BF16)
    gnh, gna = g_hgrn[0][None, :], g_attn[0][None, :]
    lng, lnb = ln_g[0][None, :], ln_b[0][None, :]

    xp = x_prompt.reshape(t, d)
    kc = 2 * LANES
    (q, kk, v, lf, gh, aq, ak, av, ag, iq, ik, iw, kbf, vbf, kit) = _project(
        xp, wm, wt, wikt, lb_logits, tm=256, kc=kc)
    rs = lambda a: a.reshape(1, t, SEG)
    s0 = jnp.zeros((1, H_HGRN, D_HEAD, D_HEAD), F32)
    o_h, s_p = _hgrn(rs(q), rs(kk), rs(v), rs(lf), s0, rows=256, t_valid=HGRN_CHUNK)
    o_a = _dsa_prompt(aq, iq, iw, kbf, vbf, kit, nsel=min(TOPK_MAX, t // 4))
    y_p = _merge(xp, o_h.reshape(t, SEG), gh, o_a, ag, gnh, gna, wo, lng, lnb, tm=512)

    rows = SAMPLE_ROWS
    xs = jnp.pad(x_sample, ((0, 0), (0, rows - ts), (0, 0))).reshape(bs * rows, d)
    (q, kk, v, lf, gh, aq, ak_s, av_s, ag, iq, ik_s, iw, _, _) = _project(
        xs, wm, wt, wikt, lb_logits, tm=bs * rows, kc=0)
    rs = lambda a: a.reshape(bs, rows, a.shape[-1])
    o_h, s_s = _hgrn(rs(q), rs(kk), rs(v), rs(lf), state_hgrn[0], rows=rows, t_valid=ts)
    nsel = min(TOPK_MAX, (n_pages * page + ts) // 4)
    kidx_t = jnp.swapaxes(cache_kidx[0], 1, 2)
    scores, thr = _sample_scores(page_table, kidx_t, rs(iq), rs(iw), rs(ik_s), nsel=nsel, t_valid=ts)
    pages = lambda c: c[0].reshape(c.shape[1], page * H_ATT, D_HEAD)
    o_a = _sample_attn(page_table, pages(cache_k), pages(cache_v), rs(aq), rs(ak_s), rs(av_s), scores, thr)
    y_s = _merge(xs, o_h.reshape(bs * rows, SEG), gh, o_a.reshape(bs * rows, SEG), ag,
                 gnh, gna, wo, lng, lnb, tm=bs * rows)

    heads = lambda a, n: a.reshape(1, 1, n, H_ATT, D_HEAD)
    sample = lambda a: a.reshape(bs, rows, -1)[:, :ts]
    return (y_p.reshape(1, t, d),
            sample(y_s),
            heads(ak, t), heads(av, t), ik.reshape(1, 1, t, D_IDX), s_p[None],
            sample(ak_s).reshape(1, bs, ts, H_ATT, D_HEAD), sample(av_s).reshape(1, bs, ts, H_ATT, D_HEAD),
            sample(ik_s)[None], s_s[None])
```

```python
import functools

import jax
import jax.numpy as jnp
from jax import lax
from jax.experimental import pallas as pl
from jax.experimental.pallas import tpu as pltpu

F32 = jnp.float32
BF16 = jnp.bfloat16
I32 = jnp.int32

H_HGRN = 4
H_ATT = 4
D_HEAD = 128
H_IDX = 8
D_IDX = 64
SEG = 512
N_SEG = 9
TOPK_MAX = 256
HGRN_CHUNK = 64
Q_BLOCK = 128
PROMPT_KEY_CHUNK = 512
EPS = 1e-5
DEPTH = 1
ALPHA = (2 * DEPTH) ** 0.25
IDX_SCALE = H_IDX ** -0.5 * D_IDX ** -0.5
ATT_SCALE = D_HEAD ** -0.5
LOG2E = 1.4426950408889634

LANES = 128
SUBLANES = 8
SAMPLE_ROWS = SUBLANES

INT_MAX = 2 ** 31 - 1
KEY_NEG_INF = -2139095041
KEY_POS_INF = 2139095040
ZERO_KEY_MIN = -(2 ** 23)
ZERO_KEY_MAX = 2 ** 23 - 1
NEG_BIG = -1e30
MAX_BISECT_STEPS = 34
TIE_GROUP_KEYS = 1

SCORE_PAGES_PER_GROUP = 32
SCORE_PAGES_PER_DOT = 8
ATTN_PAGES_PER_GROUP = 8

NT_DIMS = (((1,), (1,)), ((), ()))
TN_DIMS = (((0,), (0,)), ((), ()))


def _sigmoid(x):
    return 1.0 / (1.0 + jnp.exp(-x))


def _vmem_params(nbytes, semantics):
    return pltpu.CompilerParams(dimension_semantics=semantics, vmem_limit_bytes=int(nbytes))


def _resident(shape):
    zeros = (0,) * len(shape)
    return pl.BlockSpec(shape, lambda *_: zeros, pipeline_mode=pl.Buffered(1))


def _lane_sum(x):
    return jnp.broadcast_to(jnp.sum(x, axis=1, keepdims=True), x.shape)


def _sublane_all(op, x):
    return jnp.broadcast_to(op(x, axis=0, keepdims=True), x.shape)


def _pad_rows(a, n):
    return jnp.concatenate([a, jnp.zeros((n - a.shape[0], a.shape[1]), a.dtype)], axis=0)


def _proj_kernel(x_ref, wm_ref, wt_ref, wx_ref, lbl_ref, *out_refs, kc):
    x = x_ref[...].astype(BF16)

    def seg(j):
        return jnp.dot(x, wm_ref[:, j * SEG:(j + 1) * SEG], preferred_element_type=F32)

    q_ref, kk_ref, v_ref, lf_ref, gh_ref, ak_ref, av_ref, ag_ref, ik_ref = out_refs[:9]
    hq = seg(0)
    q_ref[...] = hq * _sigmoid(hq)
    lbl = lbl_ref[...]
    e = jnp.exp(lbl - jnp.max(lbl, axis=0, keepdims=True))
    lb = e[0:1] / jnp.sum(e, axis=0, keepdims=True)
    f = lb + (1.0 - lb) * _sigmoid(seg(1))
    kk_ref[...] = 1.0 - f
    lf_ref[...] = jnp.log(f)
    v_ref[...] = seg(2)
    gh_ref[...] = seg(3)
    ak = seg(5)
    ak_ref[...] = ak
    av_ref[...] = seg(6)
    ag_ref[...] = seg(7)
    tail = jnp.dot(x, wt_ref[...], preferred_element_type=F32)
    ik = tail[:, :D_IDX]
    ik_ref[...] = ik
    if kc == 0:
        aq_ref, iq_ref, iw_ref = out_refs[9:]
        aq_ref[...] = seg(4)
        iq_ref[...] = seg(8)
        iw_ref[...] = tail[:, D_IDX:D_IDX + H_IDX]
    else:
        kbf_ref, kib_ref, aqt_ref, qit_ref, iwt_ref, vt_ref = out_refs[9:]
        kbf_ref[...] = ak.astype(BF16)
        kib_ref[...] = ik.astype(BF16)
        xt = lax.dot_general(wx_ref[...], x, NT_DIMS, preferred_element_type=F32)
        aqt_ref[...] = (xt[:SEG] * (ATT_SCALE * LOG2E)).astype(BF16)
        qit_ref[...] = xt[SEG:2 * SEG].astype(BF16)
        iwt_ref[...] = xt[3 * SEG:] * IDX_SCALE
        vt = xt[2 * SEG:3 * SEG].astype(BF16)
        for j in range(vt.shape[1] // kc):
            vt_ref[j] = vt[:, j * kc:(j + 1) * kc]


def _project(x2, wm, wt, wx, lbl, *, tm, kc):
    t, d = x2.shape
    assert t % tm == 0 and (kc == 0 or tm % kc == 0)
    row = lambda w: pl.BlockSpec((tm, w), lambda i: (i, 0))
    col = lambda h: pl.BlockSpec((h, tm), lambda i: (0, i))
    f32 = lambda *s: jax.ShapeDtypeStruct(s, F32)
    b16 = lambda *s: jax.ShapeDtypeStruct(s, BF16)
    out_shape = [f32(t, SEG)] * 8 + [f32(t, D_IDX)]
    out_specs = [row(SEG)] * 8 + [row(D_IDX)]
    if kc == 0:
        out_shape += [f32(t, SEG), f32(t, SEG), f32(t, H_IDX)]
        out_specs += [row(SEG), row(SEG), row(H_IDX)]
    else:
        out_shape += [b16(t, SEG), b16(t, D_IDX), b16(SEG, t), b16(SEG, t), f32(H_IDX, t), b16(t // kc, SEG, kc)]
        out_specs += [row(SEG), row(D_IDX), col(SEG), col(SEG), col(H_IDX),
                      pl.BlockSpec((tm // kc, SEG, kc), lambda i: (i, 0, 0))]
    vmem = (wm.size + wx.size) * 2 + 2 * tm * d * 4 + 2 * tm * SEG * 12 * 4 + (8 << 20)
    return pl.pallas_call(
        functools.partial(_proj_kernel, kc=kc),
        grid=(t // tm,),
        in_specs=[row(d), _resident(wm.shape), _resident(wt.shape), _resident(wx.shape), _resident(lbl.shape)],
        out_specs=out_specs,
        out_shape=out_shape,
        compiler_params=_vmem_params(vmem, ("parallel",)),
        name="proj",
    )(x2, wm, wt, wx, lbl)


def _hgrn_kernel(q_ref, kk_ref, v_ref, lf_ref, s0_ref, o_ref, sout_ref, st_ref, *, rows, t_valid):
    c_len = HGRN_CHUNK
    step = pl.program_id(1)

    @pl.when(step == 0)
    def _():
        for h in range(H_HGRN):
            st_ref[h] = s0_ref[0, h].T

    r = lax.broadcasted_iota(I32, (c_len, c_len), 0)
    c = lax.broadcasted_iota(I32, (c_len, c_len), 1)
    causal = r >= c
    tri = causal.astype(F32)
    pad = c_len - rows if rows < c_len else 0

    def load(ref, r0, n, h):
        a = ref[0, r0:r0 + n, h * D_HEAD:(h + 1) * D_HEAD]
        if pad:
            a = jnp.concatenate([a, jnp.zeros((pad, D_HEAD), F32)], axis=0)
        return a

    n_rows = min(rows, c_len)
    for ci in range(max(rows // c_len, 1)):
        r0 = ci * c_len
        for h in range(H_HGRN):
            q = load(q_ref, r0, n_rows, h)
            kk = load(kk_ref, r0, n_rows, h)
            v = load(v_ref, r0, n_rows, h)
            lf = load(lf_ref, r0, n_rows, h)
            if t_valid < c_len:
                lf = jnp.where(lax.broadcasted_iota(I32, lf.shape, 0) < t_valid, lf, 0.0)
            b = jnp.dot(tri, lf, precision=lax.Precision.HIGHEST, preferred_element_type=F32)
            b_end = b[c_len - 1:c_len, :]
            qd = (q * jnp.exp(b)).astype(BF16)
            kd = (kk * jnp.exp(-b)).astype(BF16)
            k2 = (kk * jnp.exp(b_end - b)).astype(BF16)
            vb = v.astype(BF16)
            att = lax.dot_general(qd, kd, NT_DIMS, preferred_element_type=F32)
            att = jnp.where(causal, att, 0.0).astype(BF16)
            st = st_ref[h]
            o = lax.dot_general(qd, st.astype(BF16), NT_DIMS, preferred_element_type=F32)
            o = o + jnp.dot(att, vb, preferred_element_type=F32)
            st_ref[h] = st * jnp.exp(b_end) + lax.dot_general(vb, k2, TN_DIMS, preferred_element_type=F32)
            o_ref[0, r0:r0 + n_rows, h * D_HEAD:(h + 1) * D_HEAD] = o[:n_rows]

    @pl.when(step == pl.num_programs(1) - 1)
    def _():
        for h in range(H_HGRN):
            sout_ref[0, h] = st_ref[h].T


def _hgrn(q, kk, v, lf, s0, *, rows, t_valid):
    b, t, _ = q.shape
    assert t % rows == 0
    blk = pl.BlockSpec((1, rows, SEG), lambda i, j: (i, j, 0))
    sblk = pl.BlockSpec((1, H_HGRN, D_HEAD, D_HEAD), lambda i, j: (i, 0, 0, 0))
    return pl.pallas_call(
        functools.partial(_hgrn_kernel, rows=rows, t_valid=t_valid),
        grid=(b, t // rows),
        in_specs=[blk, blk, blk, blk, sblk],
        out_specs=[blk, sblk],
        out_shape=[jax.ShapeDtypeStruct((b, t, SEG), F32),
                   jax.ShapeDtypeStruct((b, H_HGRN, D_HEAD, D_HEAD), F32)],
        scratch_shapes=[pltpu.VMEM((H_HGRN, D_HEAD, D_HEAD), F32)],
        compiler_params=_vmem_params(32 << 20, ("parallel", "arbitrary")),
        name="hgrn",
    )(q, kk, v, lf, s0)


def _is_zero_key(k):
    return jnp.logical_and(k >= ZERO_KEY_MIN, k <= ZERO_KEY_MAX)


def _key_to_float(k):
    f = lax.bitcast_convert_type(k ^ (lax.shift_right_arithmetic(k, 31) & INT_MAX), F32)
    return jnp.where(_is_zero_key(k), 0.0, f)


def _next_key(k, step=1):
    return jnp.where(_is_zero_key(k), jnp.maximum(k + step, ZERO_KEY_MAX + 1), k + step)


def _tie_group(lo):
    g = TIE_GROUP_KEYS
    start = lo if g == 1 else jnp.where(lo > KEY_NEG_INF + g, lo & (-g), lo)
    return start, _next_key(jnp.minimum(start, KEY_POS_INF + 1 - g), g)


def _bisect_threshold(count_ge, nsel, lo0, hi0):
    def active(lo, hi):
        return (lo + 1) != hi

    def cond(state):
        it, lo, hi, _ = state
        return jnp.logical_and(it < MAX_BISECT_STEPS,
                               jnp.max(jnp.where(active(lo, hi), 1.0, 0.0)) > 0.0)

    def body(state):
        it, lo, hi, clo = state
        mid = (lo & hi) + lax.shift_right_arithmetic(lo ^ hi, 1)
        cnt = count_ge(_key_to_float(mid))
        ge = cnt >= nsel
        exact = cnt == nsel
        act = active(lo, hi)
        new_hi = jnp.where(exact, mid + 1, jnp.where(ge, hi, mid))
        return (it + 1,
                jnp.where(jnp.logical_and(act, ge), mid, lo),
                jnp.where(act, new_hi, hi),
                jnp.where(jnp.logical_and(act, ge), cnt, clo))

    _, lo, _, clo = lax.while_loop(cond, body, (jnp.int32(0), lo0, hi0, jnp.zeros(lo0.shape, F32)))
    return lo, clo


def _topk_threshold_rows(sc_ref, nch, nsel, rows, rows_valid):
    shape = (rows, LANES)

    def count_ge(thr):
        parts = [jnp.zeros(shape, F32)] * 4
        for ci in range(nch):
            parts[ci % 4] = parts[ci % 4] + jnp.where(sc_ref[ci] >= thr, 1.0, 0.0)
        return _lane_sum((parts[0] + parts[1]) + (parts[2] + parts[3]))

    real = lax.broadcasted_iota(I32, shape, 0) < rows_valid
    lo, clo = _bisect_threshold(count_ge, nsel, jnp.full(shape, KEY_NEG_INF, I32),
                                jnp.where(real, KEY_POS_INF + 1, KEY_NEG_INF + 1))
    lo, group_end = _tie_group(lo)
    if TIE_GROUP_KEYS > 1:
        clo = jnp.where(real, count_ge(_key_to_float(lo)), 0.0)
    t = _key_to_float(lo)

    @pl.when(jnp.max(clo) > nsel)
    def _():
        t_next = _key_to_float(group_end)
        need = nsel - count_ge(t_next)
        zero = jnp.zeros(shape, F32)

        def residual(s):
            return jnp.where(jnp.logical_and(s >= t, s < t_next), s - t, jnp.nan)

        def count_res(pred):
            return _lane_sum(functools.reduce(
                jnp.add, [jnp.where(pred(residual(sc_ref[ci])), 1.0, 0.0) for ci in range(nch)]))

        key0 = jnp.zeros(shape, I32)
        lo2, _ = lax.cond(jnp.max(count_res(lambda r: r > 0.0)) > 0.0,
                          lambda: _bisect_threshold(lambda thr: count_res(lambda r: r >= thr), need, key0,
                                                    jnp.full(shape, KEY_POS_INF + 1, I32)),
                          lambda: (key0, zero))
        t2 = _key_to_float(lo2)
        t2_next = _key_to_float(_next_key(lo2))
        need2 = need - count_res(lambda r: r >= t2_next)
        strict_upper = (lax.broadcasted_iota(I32, (LANES, LANES), 0)
                        < lax.broadcasted_iota(I32, (LANES, LANES), 1)).astype(BF16)

        def tie_body(ci, seen):
            s = sc_ref[ci]
            res = residual(s)
            tied = jnp.logical_and(res >= t2, res < t2_next)
            tiedf = jnp.where(tied, 1.0, 0.0)
            before = seen + jnp.dot(tiedf.astype(BF16), strict_upper, preferred_element_type=F32)
            lose = jnp.logical_or(res < t2, jnp.logical_and(tied, before >= need2))
            sc_ref[ci] = jnp.where(lose, jnp.nan, s)
            return seen + _lane_sum(tiedf)

        lax.fori_loop(0, nch, tie_body, zero)

    return t


def _topk_threshold_cols(sc_ref, nch, nsel, kc):
    shape = (SUBLANES, LANES)
    nv = kc // SUBLANES
    zero = jnp.zeros(shape, F32)

    def count_ge(thr):
        def body(ci, parts):
            parts = list(parts)
            for v in range(nv):
                s = sc_ref[ci, v * SUBLANES:(v + 1) * SUBLANES, :]
                parts[v % 4] = parts[v % 4] + jnp.where(s >= thr, 1.0, 0.0)
            return tuple(parts)
        parts = lax.fori_loop(0, nch, body, (zero,) * 4)
        return _sublane_all(jnp.sum, (parts[0] + parts[1]) + (parts[2] + parts[3]))

    def reduce_tiles(fn, op, init):
        def body(ci, parts):
            parts = list(parts)
            for v in range(nv):
                s = sc_ref[ci, v * SUBLANES:(v + 1) * SUBLANES, :]
                parts[v % 4] = op(parts[v % 4], fn(s))
            return tuple(parts)
        parts = lax.fori_loop(0, nch, body, (init,) * 4)
        return op(op(parts[0], parts[1]), op(parts[2], parts[3]))

    lo, clo = _bisect_threshold(count_ge, nsel, jnp.full(shape, KEY_NEG_INF, I32),
                                jnp.full(shape, KEY_POS_INF + 1, I32))
    lo, group_end = _tie_group(lo)
    if TIE_GROUP_KEYS > 1:
        clo = count_ge(_key_to_float(lo))
    t = _key_to_float(lo)

    @pl.when(jnp.max(clo) > nsel)
    def _():
        t_next = _key_to_float(group_end)
        need = nsel - count_ge(t_next)

        def residual(s, t_, t_next_):
            return jnp.where(jnp.logical_and(s >= t_, s < t_next_), s - t_, jnp.nan)

        def count_res_ge(thr):
            return _sublane_all(jnp.sum, reduce_tiles(
                lambda s: jnp.where(residual(s, t, t_next) >= thr, 1.0, 0.0), jnp.add, zero))

        rmax = _sublane_all(jnp.max, reduce_tiles(
            lambda s: jnp.where(residual(s, t, t_next) > 0.0, 1.0, 0.0), jnp.maximum, zero))
        key0 = jnp.zeros(shape, I32)
        lo2, _ = lax.cond(jnp.max(rmax) > 0.0,
                          lambda: _bisect_threshold(count_res_ge, need, key0, jnp.full(shape, KEY_POS_INF + 1, I32)),
                          lambda: (key0, zero))
        t2 = _key_to_float(lo2)
        t2_next = _key_to_float(_next_key(lo2))
        need2 = (need - count_res_ge(t2_next))[0:1]
        t_r, tn_r, t2_r, t2n_r = t[0:1], t_next[0:1], t2[0:1], t2_next[0:1]
        strict_lower = (lax.broadcasted_iota(I32, (kc, kc), 1)
                        < lax.broadcasted_iota(I32, (kc, kc), 0)).astype(BF16)

        def tie_body(ci, seen):
            s = sc_ref[ci]
            res = residual(s, t_r, tn_r)
            tied = jnp.logical_and(res >= t2_r, res < t2n_r)
            tiedf = jnp.where(tied, 1.0, 0.0)
            before = seen + jnp.dot(strict_lower, tiedf.astype(BF16), preferred_element_type=F32)
            lose = jnp.logical_or(res < t2_r, jnp.logical_and(tied, before >= need2))
            sc_ref[ci] = jnp.where(lose, jnp.nan, s)
            return seen + jnp.sum(tiedf, axis=0, keepdims=True)

        lax.fori_loop(0, nch, tie_body, jnp.zeros((1, LANES), F32))

    return t


def _dsa_prompt_kernel(aqt_ref, qit_ref, iwt_ref, kbf_ref, kib_ref, vt_ref, o_ref, sc_ref, acc_ref, *, nsel, kc):
    qb = Q_BLOCK
    vc = vt_ref.shape[2]
    nv = kc // SUBLANES
    i = pl.program_id(0)
    nch = ((i + 1) * qb + kc - 1) // kc

    qit = qit_ref[...]
    rhs = jnp.concatenate([qit[h * D_IDX:(h + 1) * D_IDX, :] for h in range(H_IDX)], axis=1)
    w = iwt_ref[...]

    def scores(ci):
        parts = []
        for j in range(kc // vc):
            r0 = pl.multiple_of(ci * kc + j * vc, vc)
            s = jnp.dot(kib_ref[pl.ds(r0, vc), :], rhs, preferred_element_type=F32)
            acc = jnp.zeros((vc, LANES), F32)
            for h in range(H_IDX):
                acc = acc + jnp.maximum(s[:, h * LANES:(h + 1) * LANES], 0.0) * w[h:h + 1, :]
            parts.append(acc)
        return jnp.concatenate(parts, axis=0)

    def score_body(ci, carry):
        sc_ref[ci] = scores(ci)
        return carry

    lax.fori_loop(0, nch - 1, score_body, 0)
    last = nch - 1
    kpos = last * kc + lax.broadcasted_iota(I32, (kc, LANES), 0)
    qpos = i * qb + lax.broadcasted_iota(I32, (kc, LANES), 1)
    sc_ref[last] = jnp.where(kpos <= qpos, scores(last), jnp.nan)

    t = _topk_threshold_cols(sc_ref, nch, nsel, kc)

    aqt = aqt_ref[...]
    qh = [aqt[h * D_HEAD:(h + 1) * D_HEAD, :] for h in range(H_ATT)]
    zq = jnp.zeros((D_HEAD, LANES), BF16)
    qpair = [jnp.concatenate([jnp.concatenate([qh[2 * p], zq], axis=1),
                              jnp.concatenate([zq, qh[2 * p + 1]], axis=1)], axis=0) for p in range(H_ATT // 2)]
    for h in range(H_ATT):
        acc_ref[h] = jnp.zeros((D_HEAD, LANES), F32)

    def tile(x):
        return jnp.broadcast_to(x[None], (nv, SUBLANES, LANES)).reshape(kc, LANES)

    def att_body(ci, carry):
        m, l = carry
        r0 = pl.multiple_of(ci * kc, kc)
        bias = jnp.where(sc_ref[ci] >= tile(t), 0.0, NEG_BIG)
        m2, l2 = [], []
        lg_pair = [jnp.dot(kbf_ref[pl.ds(r0, kc), p * 2 * D_HEAD:(p + 1) * 2 * D_HEAD], qpair[p],
                           preferred_element_type=F32) for p in range(H_ATT // 2)]
        for h in range(H_ATT):
            hs = slice(h * D_HEAD, (h + 1) * D_HEAD)
            lg = lg_pair[h // 2][:, (h % 2) * LANES:(h % 2 + 1) * LANES] + bias
            mx = jnp.max(lg.reshape(nv, SUBLANES, LANES), axis=0)
            m_new = jnp.maximum(m[h], _sublane_all(jnp.max, mx))
            alpha = jnp.exp2(m[h] - m_new)
            p = jnp.exp2(lg - tile(m_new))
            l2.append(alpha * l[h] + jnp.sum(p.reshape(nv, SUBLANES, LANES), axis=0))
            pb = p.astype(BF16)
            pv = functools.reduce(jnp.add, [
                jnp.dot(vt_ref[ci * (kc // vc) + j, hs, :], pb[j * vc:(j + 1) * vc], preferred_element_type=F32)
                for j in range(kc // vc)])
            acc_ref[h] = acc_ref[h] * jnp.broadcast_to(alpha[0:1], (D_HEAD, LANES)) + pv
            m2.append(m_new)
        return m2, l2

    init = ([jnp.full((SUBLANES, LANES), NEG_BIG, F32)] * H_ATT, [jnp.zeros((SUBLANES, LANES), F32)] * H_ATT)
    _, l = lax.fori_loop(0, nch, att_body, init)
    for h in range(H_ATT):
        denom = jnp.broadcast_to(jnp.sum(l[h], axis=0, keepdims=True), (D_HEAD, LANES))
        o_ref[:, h * D_HEAD:(h + 1) * D_HEAD] = (acc_ref[h] / denom).T


def _dsa_prompt(aqt, qit, iwt, kbf, kib, vt, *, nsel):
    t = kbf.shape[0]
    vc = vt.shape[2]
    kc = PROMPT_KEY_CHUNK
    nchunks = t // kc
    qb = Q_BLOCK
    assert t % qb == 0 and t % kc == 0 and kc % vc == 0 and vt.shape[0] * vc == t
    col = lambda h: pl.BlockSpec((h, qb), lambda i: (0, i))
    vmem = kbf.size * 2 * 2 + t * LANES * 2 + nchunks * kc * LANES * 4 + (12 << 20)
    return pl.pallas_call(
        functools.partial(_dsa_prompt_kernel, nsel=nsel, kc=kc),
        grid=(t // qb,),
        in_specs=[col(SEG), col(SEG), col(H_IDX), _resident(kbf.shape), _resident(kib.shape), _resident(vt.shape)],
        out_specs=pl.BlockSpec((qb, SEG), lambda i: (i, 0)),
        out_shape=jax.ShapeDtypeStruct((t, SEG), F32),
        scratch_shapes=[pltpu.VMEM((nchunks, kc, LANES), F32), pltpu.VMEM((H_ATT, D_HEAD, LANES), F32)],
        compiler_params=_vmem_params(vmem, ("arbitrary",)),
        name="dsa_prompt",
    )(aqt, qit, iwt, kbf, kib, vt)


def _page_group_copies(hbm_refs, bufs, sems, pt_ref, seq, group, slot, pages_per_group):
    copies = []
    for hbm, buf, sem in zip(hbm_refs, bufs, sems):
        for j in range(pages_per_group):
            page = pt_ref[seq, group * pages_per_group + j]
            copies.append(pltpu.make_async_copy(hbm.at[page], buf.at[slot, j], sem.at[slot]))
    return copies


def _stream_page_groups(hbm_refs, bufs, sems, pt_ref, n_groups, pages_per_group, consume, carry):
    assert n_groups % 2 == 0
    b = pl.program_id(0)
    n_seq = pl.num_programs(0)
    copies = functools.partial(_page_group_copies, hbm_refs, bufs, sems, pt_ref,
                               pages_per_group=pages_per_group)

    @pl.when(b == 0)
    def _():
        for c in copies(0, 0, 0):
            c.start()

    def pair_body(i, carry):
        for slot in range(2):
            g = 2 * i + slot

            @pl.when(g + 1 < n_groups)
            def _():
                for c in copies(b, g + 1, 1 - slot):
                    c.start()

            @pl.when(jnp.logical_and(g + 1 == n_groups, b + 1 < n_seq))
            def _():
                for c in copies(b + 1, 0, 1 - slot):
                    c.start()

            for c in copies(b, g, slot):
                c.wait()
            carry = consume(g, slot, carry)
        return carry

    return lax.fori_loop(0, n_groups // 2, pair_body, carry)


def _sample_scores_kernel(pt_ref, kidx_hbm, qi_ref, wi_ref, kin_ref, sc_ref, thr_ref, kbuf, sem,
                          *, n_pages, nsel, t_valid):
    rows = SAMPLE_ROWS
    gp, dp = SCORE_PAGES_PER_GROUP, SCORE_PAGES_PER_DOT
    qi = qi_ref[0]
    qhm = jnp.concatenate([qi[:, h * D_IDX:(h + 1) * D_IDX] for h in range(H_IDX)], axis=0).astype(BF16)
    wi = wi_ref[0] * IDX_SCALE

    def head_sum(s):
        acc = jnp.zeros((rows, s.shape[1]), F32)
        for h in range(H_IDX):
            acc = acc + jnp.maximum(s[h * rows:(h + 1) * rows], 0.0) * wi[:, h:h + 1]
        return acc

    def consume(g, slot, carry):
        for d in range(gp // dp):
            kt = jnp.concatenate([kbuf[slot, d * dp + j] for j in range(dp)], axis=1).astype(BF16)
            acc = head_sum(jnp.dot(qhm, kt, preferred_element_type=F32))
            for j in range(dp):
                sc_ref[0, g * gp + d * dp + j] = acc[:, j * LANES:(j + 1) * LANES]
        return carry

    _stream_page_groups([kidx_hbm], [kbuf], [sem], pt_ref, n_pages // gp, gp, consume, 0)

    kn = _pad_rows(kin_ref[0], LANES).astype(BF16)
    r = lax.broadcasted_iota(I32, (rows, LANES), 0)
    c = lax.broadcasted_iota(I32, (rows, LANES), 1)
    acc = head_sum(lax.dot_general(qhm, kn, NT_DIMS, preferred_element_type=F32))
    sc_ref[0, n_pages] = jnp.where(jnp.logical_and(c <= r, c < t_valid), acc, jnp.nan)
    thr_ref[0] = _topk_threshold_rows(sc_ref.at[0], n_pages + 1, nsel, rows, t_valid)


def _sample_scores(page_table, kidx_t, qi, wi, ki_new, *, nsel, t_valid):
    b, n_pages = page_table.shape
    _, _, page = kidx_t.shape
    gp = SCORE_PAGES_PER_GROUP
    assert page == LANES and n_pages % (2 * gp) == 0
    rows = SAMPLE_ROWS
    seq = lambda *tail: pl.BlockSpec((1,) + tail, lambda i, pt: (i,) + (0,) * len(tail))
    grid_spec = pltpu.PrefetchScalarGridSpec(
        num_scalar_prefetch=1,
        grid=(b,),
        in_specs=[pl.BlockSpec(memory_space=pl.ANY), seq(rows, SEG), seq(rows, H_IDX), seq(rows, D_IDX)],
        out_specs=[seq(n_pages + 1, rows, page), seq(rows, LANES)],
        scratch_shapes=[pltpu.VMEM((2, gp, D_IDX, page), F32), pltpu.SemaphoreType.DMA((2,))],
    )
    return pl.pallas_call(
        functools.partial(_sample_scores_kernel, n_pages=n_pages, nsel=nsel, t_valid=t_valid),
        grid_spec=grid_spec,
        out_shape=[jax.ShapeDtypeStruct((b, n_pages + 1, rows, page), F32),
                   jax.ShapeDtypeStruct((b, rows, LANES), F32)],
        compiler_params=_vmem_params(32 << 20, ("arbitrary",)),
        name="sample_scores",
    )(page_table, kidx_t, qi, wi, ki_new)


def _sample_attn_kernel(pt_ref, k_hbm, v_hbm, aq_ref, kn_ref, vn_ref, sc_ref, thr_ref, o_ref,
                        kbuf, vbuf, ksem, vsem, *, n_pages):
    rows = SAMPLE_ROWS
    gp = ATTN_PAGES_PER_GROUP
    page = LANES
    thr = thr_ref[0]
    aq = aq_ref[0]
    qh = [aq[:, h * D_HEAD:(h + 1) * D_HEAD].astype(BF16) for h in range(H_ATT)]

    def head_rows(buf, slot, j, h):
        return buf[slot, j, pl.ds(h, page, stride=H_ATT), :].astype(BF16)

    def attend(carry, sel, keys, values):
        m, l, acc = carry
        n = len(sel)
        m2, l2, acc2 = [], [], []
        for h in range(H_ATT):
            lg = [jnp.where(sel[j], lax.dot_general(qh[h], keys[j][h], NT_DIMS,
                                                    preferred_element_type=F32) * ATT_SCALE, NEG_BIG)
                  for j in range(n)]
            mx = functools.reduce(jnp.maximum, lg)
            m_new = jnp.maximum(m[h], jnp.broadcast_to(jnp.max(mx, axis=1, keepdims=True), mx.shape))
            alpha = jnp.exp(m[h] - m_new)
            p = [jnp.where(sel[j], jnp.exp(lg[j] - m_new), 0.0) for j in range(n)]
            pv = functools.reduce(jnp.add, [jnp.dot(p[j].astype(BF16), values[j][h],
                                                    preferred_element_type=F32) for j in range(n)])
            m2.append(m_new)
            l2.append(alpha * l[h] + _lane_sum(functools.reduce(jnp.add, p)))
            acc2.append(alpha * acc[h] + pv)
        return m2, l2, acc2

    def consume(g, slot, carry):
        sel = [sc_ref[0, g * gp + j] >= thr for j in range(gp)]
        keys = [[head_rows(kbuf, slot, j, h) for h in range(H_ATT)] for j in range(gp)]
        values = [[head_rows(vbuf, slot, j, h) for h in range(H_ATT)] for j in range(gp)]
        return attend(carry, sel, keys, values)

    init = ([jnp.full((rows, LANES), NEG_BIG, F32)] * H_ATT, [jnp.zeros((rows, LANES), F32)] * H_ATT,
            [jnp.zeros((rows, LANES), F32)] * H_ATT)
    carry = _stream_page_groups([k_hbm, v_hbm], [kbuf, vbuf], [ksem, vsem], pt_ref, n_pages // gp, gp,
                                consume, init)
    kn, vn = kn_ref[0], vn_ref[0]
    hs = lambda h: slice(h * D_HEAD, (h + 1) * D_HEAD)
    _, l, acc = attend(carry, [sc_ref[0, n_pages] >= thr],
                       [[_pad_rows(kn[:, hs(h)], page).astype(BF16) for h in range(H_ATT)]],
                       [[_pad_rows(vn[:, hs(h)], page).astype(BF16) for h in range(H_ATT)]])
    for h in range(H_ATT):
        o_ref[0, :, hs(h)] = acc[h] / l[h]


def _sample_attn(page_table, cache_k, cache_v, aq, k_new, v_new, scores, thr):
    b, n_pages = page_table.shape
    gp = ATTN_PAGES_PER_GROUP
    rows = SAMPLE_ROWS
    page_rows = cache_k.shape[1]
    assert page_rows == LANES * H_ATT and n_pages % (2 * gp) == 0
    seq = lambda *tail: pl.BlockSpec((1,) + tail, lambda i, pt: (i,) + (0,) * len(tail))
    buf = pltpu.VMEM((2, gp, page_rows, D_HEAD), F32)
    grid_spec = pltpu.PrefetchScalarGridSpec(
        num_scalar_prefetch=1,
        grid=(b,),
        in_specs=[pl.BlockSpec(memory_space=pl.ANY), pl.BlockSpec(memory_space=pl.ANY),
                  seq(rows, SEG), seq(rows, SEG), seq(rows, SEG), seq(n_pages + 1, rows, LANES), seq(rows, LANES)],
        out_specs=seq(rows, SEG),
        scratch_shapes=[buf, buf, pltpu.SemaphoreType.DMA((2,)), pltpu.SemaphoreType.DMA((2,))],
    )
    vmem = 2 * 2 * gp * page_rows * D_HEAD * 4 + (16 << 20)
    return pl.pallas_call(
        functools.partial(_sample_attn_kernel, n_pages=n_pages),
        grid_spec=grid_spec,
        out_shape=jax.ShapeDtypeStruct((b, rows, SEG), F32),
        compiler_params=_vmem_params(vmem, ("arbitrary",)),
        name="sample_attn",
    )(page_table, cache_k, cache_v, aq, k_new, v_new, scores, thr)


def _merge_kernel(x_ref, oh_ref, gh_ref, oa_ref, ga_ref, gnh_ref, gna_ref, wo_ref, lng_ref, lnb_ref, y_ref):
    def branch(o_ref, gate_ref, gain_ref):
        parts = []
        for h in range(H_ATT):
            o = o_ref[:, h * D_HEAD:(h + 1) * D_HEAD]
            parts.append(o * lax.rsqrt(jnp.mean(o * o, axis=-1, keepdims=True) + EPS))
        g = gate_ref[...]
        return (jnp.concatenate(parts, axis=1) * gain_ref[...]) * (g * _sigmoid(g))

    hh = branch(oh_ref, gh_ref, gnh_ref).astype(BF16)
    aa = branch(oa_ref, ga_ref, gna_ref).astype(BF16)
    mix = (jnp.dot(hh, wo_ref[:SEG, :], preferred_element_type=F32)
           + jnp.dot(aa, wo_ref[SEG:, :], preferred_element_type=F32))
    r = ALPHA * x_ref[...] + mix
    mu = jnp.mean(r, axis=-1, keepdims=True)
    rc = r - mu
    var = jnp.mean(rc * rc, axis=-1, keepdims=True)
    y_ref[...] = rc * lax.rsqrt(var + EPS) * lng_ref[...] + lnb_ref[...]


def _merge(x2, oh, gh, oa, ga, gnh, gna, wo, lng, lnb, *, tm):
    t, d = x2.shape
    assert t % tm == 0
    row = lambda w: pl.BlockSpec((tm, w), lambda i: (i, 0))
    return pl.pallas_call(
        _merge_kernel,
        grid=(t // tm,),
        in_specs=[row(d), row(SEG), row(SEG), row(SEG), row(SEG), _resident(gnh.shape), _resident(gna.shape),
                  _resident(wo.shape), _resident(lng.shape), _resident(lnb.shape)],
        out_specs=row(d),
        out_shape=jax.ShapeDtypeStruct((t, d), F32),
        compiler_params=_vmem_params(40 << 20, ("parallel",)),
        name="merge",
    )(x2, oh, gh, oa, ga, gnh, gna, wo, lng, lnb)


def kernel(x_prompt, x_sample, cache_k, cache_v, cache_kidx, state_hgrn, page_table,
           w_in, w_out, lb_logits, g_hgrn, g_attn, ln_g, ln_b):
    bp, t, d = x_prompt.shape
    bs, ts, _ = x_sample.shape
    n_pages = page_table.shape[1]
    page = cache_k.shape[2]
    assert w_in.shape[0] == DEPTH and bp == 1 and ts <= SAMPLE_ROWS
    main = N_SEG * SEG

    w = w_in[0]
    wm = w[:, :main].astype(BF16)
    wt = jnp.pad(w[:, main:], ((0, 0), (0, LANES - (D_IDX + H_IDX)))).astype(BF16)
    seg_cols = lambda j: w[:, j * SEG:(j + 1) * SEG]
    wx = jnp.concatenate([seg_cols(4), seg_cols(8), seg_cols(6), w[:, main + D_IDX:]], axis=1).T.astype(BF16)
    wo = w_out[0].astype(BF16)
    gnh, gna = g_hgrn[0][None, :], g_attn[0][None, :]
    lng, lnb = ln_g[0][None, :], ln_b[0][None, :]

    xp = x_prompt.reshape(t, d)
    kc = 2 * LANES
    (q, kk, v, lf, gh, ak, av, ag, ik, kbf, kib, aqt, qit, iwt, vt) = _project(
        xp, wm, wt, wx, lb_logits, tm=256, kc=kc)
    rs = lambda a: a.reshape(1, t, SEG)
    s0 = jnp.zeros((1, H_HGRN, D_HEAD, D_HEAD), F32)
    o_h, s_p = _hgrn(rs(q), rs(kk), rs(v), rs(lf), s0, rows=256, t_valid=HGRN_CHUNK)
    o_a = _dsa_prompt(aqt, qit, iwt, kbf, kib, vt, nsel=min(TOPK_MAX, t // 4))
    y_p = _merge(xp, o_h.reshape(t, SEG), gh, o_a, ag, gnh, gna, wo, lng, lnb, tm=512)

    rows = SAMPLE_ROWS
    xs = jnp.pad(x_sample, ((0, 0), (0, rows - ts), (0, 0))).reshape(bs * rows, d)
    (q, kk, v, lf, gh, ak_s, av_s, ag, ik_s, aq, iq, iw) = _project(
        xs, wm, wt, wx, lb_logits, tm=bs * rows, kc=0)
    rs = lambda a: a.reshape(bs, rows, a.shape[-1])
    o_h, s_s = _hgrn(rs(q), rs(kk), rs(v), rs(lf), state_hgrn[0], rows=rows, t_valid=ts)
    nsel = min(TOPK_MAX, (n_pages * page + ts) // 4)
    kidx_t = jnp.swapaxes(cache_kidx[0], 1, 2)
    scores, thr = _sample_scores(page_table, kidx_t, rs(iq), rs(iw), rs(ik_s), nsel=nsel, t_valid=ts)
    pages = lambda c: c[0].reshape(c.shape[1], page * H_ATT, D_HEAD)
    o_a = _sample_attn(page_table, pages(cache_k), pages(cache_v), rs(aq), rs(ak_s), rs(av_s), scores, thr)
    y_s = _merge(xs, o_h.reshape(bs * rows, SEG), gh, o_a.reshape(bs * rows, SEG), ag,
                 gnh, gna, wo, lng, lnb, tm=bs * rows)

    heads = lambda a, n: a.reshape(1, 1, n, H_ATT, D_HEAD)
    sample = lambda a: a.reshape(bs, rows, -1)[:, :ts]
    return (y_p.reshape(1, t, d),
            sample(y_s),
            heads(ak, t), heads(av, t), ik.reshape(1, 1, t, D_IDX), s_p[None],
            sample(ak_s).reshape(1, bs, ts, H_ATT, D_HEAD), sample(av_s).reshape(1, bs, ts, H_ATT, D_HEAD),
            sample(ik_s)[None], s_s[None])
```

```python
import functools

import jax
import jax.numpy as jnp
from jax import lax
from jax.experimental import pallas as pl
from jax.experimental.pallas import tpu as pltpu

F32 = jnp.float32
BF16 = jnp.bfloat16
I32 = jnp.int32

H_HGRN = 4
H_ATT = 4
D_HEAD = 128
H_IDX = 8
D_IDX = 64
SEG = 512
N_SEG = 9
TOPK_MAX = 256
HGRN_CHUNK = 64
Q_BLOCK = 128
PROMPT_KEY_CHUNK = 512
EPS = 1e-5
DEPTH = 1
ALPHA = (2 * DEPTH) ** 0.25
IDX_SCALE = H_IDX ** -0.5 * D_IDX ** -0.5
ATT_SCALE = D_HEAD ** -0.5
LOG2E = 1.4426950408889634

LANES = 128
SUBLANES = 8
SAMPLE_ROWS = SUBLANES

INT_MAX = 2 ** 31 - 1
KEY_NEG_INF = -2139095041
KEY_POS_INF = 2139095040
ZERO_KEY_MIN = -(2 ** 23)
ZERO_KEY_MAX = 2 ** 23 - 1
NEG_BIG = -1e30
MAX_BISECT_STEPS = 34
TIE_GROUP_KEYS = 1

SCORE_PAGES_PER_GROUP = 32
SCORE_PAGES_PER_DOT = 8
ATTN_PAGES_PER_GROUP = 8

NT_DIMS = (((1,), (1,)), ((), ()))
TN_DIMS = (((0,), (0,)), ((), ()))


def _sigmoid(x):
    return 1.0 / (1.0 + jnp.exp(-x))


def _vmem_params(nbytes, semantics):
    return pltpu.CompilerParams(dimension_semantics=semantics, vmem_limit_bytes=int(nbytes))


def _resident(shape):
    zeros = (0,) * len(shape)
    return pl.BlockSpec(shape, lambda *_: zeros, pipeline_mode=pl.Buffered(1))


def _lane_sum(x):
    return jnp.broadcast_to(jnp.sum(x, axis=1, keepdims=True), x.shape)


def _sublane_all(op, x):
    return jnp.broadcast_to(op(x, axis=0, keepdims=True), x.shape)


def _pad_rows(a, n):
    return jnp.concatenate([a, jnp.zeros((n - a.shape[0], a.shape[1]), a.dtype)], axis=0)


def _proj_kernel(x_ref, wm_ref, wt_ref, wx_ref, lbl_ref, *out_refs, kc):
    x = x_ref[...].astype(BF16)

    def seg(j):
        return jnp.dot(x, wm_ref[:, j * SEG:(j + 1) * SEG], preferred_element_type=F32)

    q_ref, kk_ref, v_ref, lf_ref, gh_ref, ak_ref, av_ref, ag_ref, ik_ref = out_refs[:9]
    hq = seg(0)
    q_ref[...] = hq * _sigmoid(hq)
    lbl = lbl_ref[...]
    e = jnp.exp(lbl - jnp.max(lbl, axis=0, keepdims=True))
    lb = e[0:1] / jnp.sum(e, axis=0, keepdims=True)
    f = lb + (1.0 - lb) * _sigmoid(seg(1))
    kk_ref[...] = 1.0 - f
    lf_ref[...] = jnp.log(f)
    v_ref[...] = seg(2)
    gh_ref[...] = seg(3)
    ak = seg(5)
    ak_ref[...] = ak
    av_ref[...] = seg(6)
    ag_ref[...] = seg(7)
    tail = jnp.dot(x, wt_ref[...], preferred_element_type=F32)
    ik = tail[:, :D_IDX]
    ik_ref[...] = ik
    if kc == 0:
        aq_ref, iq_ref, iw_ref = out_refs[9:]
        aq_ref[...] = seg(4)
        iq_ref[...] = seg(8)
        iw_ref[...] = tail[:, D_IDX:D_IDX + H_IDX]
    else:
        kbf_ref, kib_ref, aqt_ref, qit_ref, iwt_ref, vt_ref = out_refs[9:]
        kbf_ref[...] = ak.astype(BF16)
        kib_ref[...] = ik.astype(BF16)
        xt = lax.dot_general(wx_ref[...], x, NT_DIMS, preferred_element_type=F32)
        aqt_ref[...] = (xt[:SEG] * (ATT_SCALE * LOG2E)).astype(BF16)
        qit_ref[...] = xt[SEG:2 * SEG].astype(BF16)
        iwt_ref[...] = xt[3 * SEG:] * IDX_SCALE
        vt = xt[2 * SEG:3 * SEG].astype(BF16)
        for j in range(vt.shape[1] // kc):
            vt_ref[j] = vt[:, j * kc:(j + 1) * kc]


def _project(x2, wm, wt, wx, lbl, *, tm, kc):
    t, d = x2.shape
    assert t % tm == 0 and (kc == 0 or tm % kc == 0)
    row = lambda w: pl.BlockSpec((tm, w), lambda i: (i, 0))
    col = lambda h: pl.BlockSpec((h, tm), lambda i: (0, i))
    f32 = lambda *s: jax.ShapeDtypeStruct(s, F32)
    b16 = lambda *s: jax.ShapeDtypeStruct(s, BF16)
    out_shape = [f32(t, SEG)] * 8 + [f32(t, D_IDX)]
    out_specs = [row(SEG)] * 8 + [row(D_IDX)]
    if kc == 0:
        out_shape += [f32(t, SEG), f32(t, SEG), f32(t, H_IDX)]
        out_specs += [row(SEG), row(SEG), row(H_IDX)]
    else:
        out_shape += [b16(t, SEG), b16(t, D_IDX), b16(SEG, t), b16(SEG, t), f32(H_IDX, t), b16(t // kc, SEG, kc)]
        out_specs += [row(SEG), row(D_IDX), col(SEG), col(SEG), col(H_IDX),
                      pl.BlockSpec((tm // kc, SEG, kc), lambda i: (i, 0, 0))]
    vmem = (wm.size + wx.size) * 2 + 2 * tm * d * 4 + 2 * tm * SEG * 12 * 4 + (8 << 20)
    return pl.pallas_call(
        functools.partial(_proj_kernel, kc=kc),
        grid=(t // tm,),
        in_specs=[row(d), _resident(wm.shape), _resident(wt.shape), _resident(wx.shape), _resident(lbl.shape)],
        out_specs=out_specs,
        out_shape=out_shape,
        compiler_params=_vmem_params(vmem, ("parallel",)),
        name="proj",
    )(x2, wm, wt, wx, lbl)


def _hgrn_kernel(q_ref, kk_ref, v_ref, lf_ref, s0_ref, o_ref, sout_ref, st_ref, *, rows, t_valid):
    c_len = HGRN_CHUNK
    step = pl.program_id(1)

    @pl.when(step == 0)
    def _():
        for h in range(H_HGRN):
            st_ref[h] = s0_ref[0, h].T

    r = lax.broadcasted_iota(I32, (c_len, c_len), 0)
    c = lax.broadcasted_iota(I32, (c_len, c_len), 1)
    causal = r >= c
    tri = causal.astype(F32)
    pad = c_len - rows if rows < c_len else 0

    def load(ref, r0, n, h):
        a = ref[0, r0:r0 + n, h * D_HEAD:(h + 1) * D_HEAD]
        if pad:
            a = jnp.concatenate([a, jnp.zeros((pad, D_HEAD), F32)], axis=0)
        return a

    n_rows = min(rows, c_len)
    n_chunks = max(rows // c_len, 1)
    tiles = [(ci, h) for ci in range(n_chunks) for h in range(H_HGRN)]

    def log_decay(ci, h):
        lf = load(lf_ref, ci * c_len, n_rows, h)
        if t_valid < c_len:
            lf = jnp.where(lax.broadcasted_iota(I32, lf.shape, 0) < t_valid, lf, 0.0)
        return jnp.dot(tri, lf, precision=lax.Precision.HIGHEST, preferred_element_type=F32)

    b = {k: log_decay(*k) for k in tiles}
    b_end = {k: b[k][c_len - 1:c_len, :] for k in tiles}
    qd = {(ci, h): (load(q_ref, ci * c_len, n_rows, h) * jnp.exp(b[ci, h])).astype(BF16) for ci, h in tiles}
    kk = {(ci, h): load(kk_ref, ci * c_len, n_rows, h) for ci, h in tiles}
    kd = {k: (kk[k] * jnp.exp(-b[k])).astype(BF16) for k in tiles}
    k2 = {k: (kk[k] * jnp.exp(b_end[k] - b[k])).astype(BF16) for k in tiles}
    vb = {(ci, h): load(v_ref, ci * c_len, n_rows, h).astype(BF16) for ci, h in tiles}
    att = {k: lax.dot_general(qd[k], kd[k], NT_DIMS, preferred_element_type=F32) for k in tiles}
    att = {k: jnp.where(causal, att[k], 0.0).astype(BF16) for k in tiles}
    kv = {k: lax.dot_general(vb[k], k2[k], TN_DIMS, preferred_element_type=F32) for k in tiles}
    o_intra = {k: jnp.dot(att[k], vb[k], preferred_element_type=F32) for k in tiles}
    pre = {k: (qd[k], o_intra[k], kv[k], jnp.exp(b_end[k])) for k in tiles}
    for h in range(H_HGRN):
        st = st_ref[h]
        for ci in range(n_chunks):
            qd, o_intra, kv, decay = pre[ci, h]
            o = o_intra + lax.dot_general(qd, st.astype(BF16), NT_DIMS, preferred_element_type=F32)
            o_ref[0, ci * c_len:ci * c_len + n_rows, h * D_HEAD:(h + 1) * D_HEAD] = o[:n_rows]
            st = st * decay + kv
        st_ref[h] = st

    @pl.when(step == pl.num_programs(1) - 1)
    def _():
        for h in range(H_HGRN):
            sout_ref[0, h] = st_ref[h].T


def _hgrn(q, kk, v, lf, s0, *, rows, t_valid):
    b, t, _ = q.shape
    assert t % rows == 0
    blk = pl.BlockSpec((1, rows, SEG), lambda i, j: (i, j, 0))
    sblk = pl.BlockSpec((1, H_HGRN, D_HEAD, D_HEAD), lambda i, j: (i, 0, 0, 0))
    return pl.pallas_call(
        functools.partial(_hgrn_kernel, rows=rows, t_valid=t_valid),
        grid=(b, t // rows),
        in_specs=[blk, blk, blk, blk, sblk],
        out_specs=[blk, sblk],
        out_shape=[jax.ShapeDtypeStruct((b, t, SEG), F32),
                   jax.ShapeDtypeStruct((b, H_HGRN, D_HEAD, D_HEAD), F32)],
        scratch_shapes=[pltpu.VMEM((H_HGRN, D_HEAD, D_HEAD), F32)],
        compiler_params=_vmem_params(32 << 20, ("parallel", "arbitrary")),
        name="hgrn",
    )(q, kk, v, lf, s0)


def _is_zero_key(k):
    return jnp.logical_and(k >= ZERO_KEY_MIN, k <= ZERO_KEY_MAX)


def _key_to_float(k):
    f = lax.bitcast_convert_type(k ^ (lax.shift_right_arithmetic(k, 31) & INT_MAX), F32)
    return jnp.where(_is_zero_key(k), 0.0, f)


def _next_key(k, step=1):
    return jnp.where(_is_zero_key(k), jnp.maximum(k + step, ZERO_KEY_MAX + 1), k + step)


def _tie_group(lo):
    g = TIE_GROUP_KEYS
    start = lo if g == 1 else jnp.where(lo > KEY_NEG_INF + g, lo & (-g), lo)
    return start, _next_key(jnp.minimum(start, KEY_POS_INF + 1 - g), g)


def _bisect_threshold(count_ge, nsel, lo0, hi0):
    def active(lo, hi):
        return (lo + 1) != hi

    def cond(state):
        it, lo, hi, _ = state
        return jnp.logical_and(it < MAX_BISECT_STEPS,
                               jnp.max(jnp.where(active(lo, hi), 1.0, 0.0)) > 0.0)

    def body(state):
        it, lo, hi, clo = state
        mid = (lo & hi) + lax.shift_right_arithmetic(lo ^ hi, 1)
        cnt = count_ge(_key_to_float(mid))
        ge = cnt >= nsel
        exact = cnt == nsel
        act = active(lo, hi)
        new_hi = jnp.where(exact, mid + 1, jnp.where(ge, hi, mid))
        return (it + 1,
                jnp.where(jnp.logical_and(act, ge), mid, lo),
                jnp.where(act, new_hi, hi),
                jnp.where(jnp.logical_and(act, ge), cnt, clo))

    _, lo, _, clo = lax.while_loop(cond, body, (jnp.int32(0), lo0, hi0, jnp.zeros(lo0.shape, F32)))
    return lo, clo


def _topk_threshold_rows(sc_ref, nch, nsel, rows, rows_valid):
    shape = (rows, LANES)

    def count_ge(thr):
        parts = [jnp.zeros(shape, F32)] * 4
        for ci in range(nch):
            parts[ci % 4] = parts[ci % 4] + jnp.where(sc_ref[ci] >= thr, 1.0, 0.0)
        return _lane_sum((parts[0] + parts[1]) + (parts[2] + parts[3]))

    real = lax.broadcasted_iota(I32, shape, 0) < rows_valid
    lo, clo = _bisect_threshold(count_ge, nsel, jnp.full(shape, KEY_NEG_INF, I32),
                                jnp.where(real, KEY_POS_INF + 1, KEY_NEG_INF + 1))
    lo, group_end = _tie_group(lo)
    if TIE_GROUP_KEYS > 1:
        clo = jnp.where(real, count_ge(_key_to_float(lo)), 0.0)
    t = _key_to_float(lo)

    @pl.when(jnp.max(clo) > nsel)
    def _():
        t_next = _key_to_float(group_end)
        need = nsel - count_ge(t_next)
        zero = jnp.zeros(shape, F32)

        def residual(s):
            return jnp.where(jnp.logical_and(s >= t, s < t_next), s - t, jnp.nan)

        def count_res(pred):
            return _lane_sum(functools.reduce(
                jnp.add, [jnp.where(pred(residual(sc_ref[ci])), 1.0, 0.0) for ci in range(nch)]))

        key0 = jnp.zeros(shape, I32)
        lo2, _ = lax.cond(jnp.max(count_res(lambda r: r > 0.0)) > 0.0,
                          lambda: _bisect_threshold(lambda thr: count_res(lambda r: r >= thr), need, key0,
                                                    jnp.full(shape, KEY_POS_INF + 1, I32)),
                          lambda: (key0, zero))
        t2 = _key_to_float(lo2)
        t2_next = _key_to_float(_next_key(lo2))
        need2 = need - count_res(lambda r: r >= t2_next)
        strict_upper = (lax.broadcasted_iota(I32, (LANES, LANES), 0)
                        < lax.broadcasted_iota(I32, (LANES, LANES), 1)).astype(BF16)

        def tie_body(ci, seen):
            s = sc_ref[ci]
            res = residual(s)
            tied = jnp.logical_and(res >= t2, res < t2_next)
            tiedf = jnp.where(tied, 1.0, 0.0)
            before = seen + jnp.dot(tiedf.astype(BF16), strict_upper, preferred_element_type=F32)
            lose = jnp.logical_or(res < t2, jnp.logical_and(tied, before >= need2))
            sc_ref[ci] = jnp.where(lose, jnp.nan, s)
            return seen + _lane_sum(tiedf)

        lax.fori_loop(0, nch, tie_body, zero)

    return t


def _topk_threshold_cols(sc_ref, nch, nsel, kc):
    shape = (SUBLANES, LANES)
    nv = kc // SUBLANES
    zero = jnp.zeros(shape, F32)

    def count_ge(thr):
        def body(ci, parts):
            parts = list(parts)
            for v in range(nv):
                s = sc_ref[ci, v * SUBLANES:(v + 1) * SUBLANES, :]
                parts[v % 4] = parts[v % 4] + jnp.where(s >= thr, 1.0, 0.0)
            return tuple(parts)
        parts = lax.fori_loop(0, nch, body, (zero,) * 4)
        return _sublane_all(jnp.sum, (parts[0] + parts[1]) + (parts[2] + parts[3]))

    def reduce_tiles(fn, op, init):
        def body(ci, parts):
            parts = list(parts)
            for v in range(nv):
                s = sc_ref[ci, v * SUBLANES:(v + 1) * SUBLANES, :]
                parts[v % 4] = op(parts[v % 4], fn(s))
            return tuple(parts)
        parts = lax.fori_loop(0, nch, body, (init,) * 4)
        return op(op(parts[0], parts[1]), op(parts[2], parts[3]))

    lo, clo = _bisect_threshold(count_ge, nsel, jnp.full(shape, KEY_NEG_INF, I32),
                                jnp.full(shape, KEY_POS_INF + 1, I32))
    lo, group_end = _tie_group(lo)
    if TIE_GROUP_KEYS > 1:
        clo = count_ge(_key_to_float(lo))
    t = _key_to_float(lo)

    @pl.when(jnp.max(clo) > nsel)
    def _():
        t_next = _key_to_float(group_end)
        need = nsel - count_ge(t_next)

        def residual(s, t_, t_next_):
            return jnp.where(jnp.logical_and(s >= t_, s < t_next_), s - t_, jnp.nan)

        def count_res_ge(thr):
            return _sublane_all(jnp.sum, reduce_tiles(
                lambda s: jnp.where(residual(s, t, t_next) >= thr, 1.0, 0.0), jnp.add, zero))

        rmax = _sublane_all(jnp.max, reduce_tiles(
            lambda s: jnp.where(residual(s, t, t_next) > 0.0, 1.0, 0.0), jnp.maximum, zero))
        key0 = jnp.zeros(shape, I32)
        lo2, _ = lax.cond(jnp.max(rmax) > 0.0,
                          lambda: _bisect_threshold(count_res_ge, need, key0, jnp.full(shape, KEY_POS_INF + 1, I32)),
                          lambda: (key0, zero))
        t2 = _key_to_float(lo2)
        t2_next = _key_to_float(_next_key(lo2))
        need2 = (need - count_res_ge(t2_next))[0:1]
        t_r, tn_r, t2_r, t2n_r = t[0:1], t_next[0:1], t2[0:1], t2_next[0:1]
        strict_lower = (lax.broadcasted_iota(I32, (kc, kc), 1)
                        < lax.broadcasted_iota(I32, (kc, kc), 0)).astype(BF16)

        def tie_body(ci, seen):
            s = sc_ref[ci]
            res = residual(s, t_r, tn_r)
            tied = jnp.logical_and(res >= t2_r, res < t2n_r)
            tiedf = jnp.where(tied, 1.0, 0.0)
            before = seen + jnp.dot(strict_lower, tiedf.astype(BF16), preferred_element_type=F32)
            lose = jnp.logical_or(res < t2_r, jnp.logical_and(tied, before >= need2))
            sc_ref[ci] = jnp.where(lose, jnp.nan, s)
            return seen + jnp.sum(tiedf, axis=0, keepdims=True)

        lax.fori_loop(0, nch, tie_body, jnp.zeros((1, LANES), F32))

    return t


def _dsa_prompt_kernel(aqt_ref, qit_ref, iwt_ref, kbf_ref, kib_ref, vt_ref, o_ref, sc_ref, acc_ref, *, nsel, kc):
    qb = Q_BLOCK
    vc = vt_ref.shape[2]
    nv = kc // SUBLANES
    i = pl.program_id(0)
    nch = 2 * (((i + 1) * qb + 2 * kc - 1) // (2 * kc))
    n_full = (i * qb + 1) // kc

    qit = qit_ref[...]
    rhs = jnp.concatenate([qit[h * D_IDX:(h + 1) * D_IDX, :] for h in range(H_IDX)], axis=1)
    w = iwt_ref[...]

    def scores(ci):
        parts = []
        for j in range(kc // vc):
            r0 = pl.multiple_of(ci * kc + j * vc, vc)
            s = jnp.dot(kib_ref[pl.ds(r0, vc), :], rhs, preferred_element_type=F32)
            acc = jnp.zeros((vc, LANES), F32)
            for h in range(H_IDX):
                acc = acc + jnp.maximum(s[:, h * LANES:(h + 1) * LANES], 0.0) * w[h:h + 1, :]
            parts.append(acc)
        return jnp.concatenate(parts, axis=0)

    def score_body(ci, carry):
        sc_ref[ci] = scores(ci)
        return carry

    def masked_score_body(ci, carry):
        kpos = ci * kc + lax.broadcasted_iota(I32, (kc, LANES), 0)
        qpos = i * qb + lax.broadcasted_iota(I32, (kc, LANES), 1)
        sc_ref[ci] = jnp.where(kpos <= qpos, scores(ci), jnp.nan)
        return carry

    lax.fori_loop(0, n_full, score_body, 0)
    lax.fori_loop(n_full, nch, masked_score_body, 0)

    t = _topk_threshold_cols(sc_ref, nch, nsel, kc)

    aqt = aqt_ref[...]
    qh = [aqt[h * D_HEAD:(h + 1) * D_HEAD, :] for h in range(H_ATT)]
    zq = jnp.zeros((D_HEAD, LANES), BF16)
    qpair = [jnp.concatenate([jnp.concatenate([qh[2 * p], zq], axis=1),
                              jnp.concatenate([zq, qh[2 * p + 1]], axis=1)], axis=0) for p in range(H_ATT // 2)]
    for h in range(H_ATT):
        acc_ref[h] = jnp.zeros((D_HEAD, LANES), F32)

    def tile(x):
        return jnp.broadcast_to(x[None], (nv, SUBLANES, LANES)).reshape(kc, LANES)

    heads = range(H_ATT)

    def logits(ci):
        r0 = pl.multiple_of(ci * kc, kc)
        return [jnp.dot(kbf_ref[pl.ds(r0, kc), p * 2 * D_HEAD:(p + 1) * 2 * D_HEAD], qpair[p],
                        preferred_element_type=F32) for p in range(H_ATT // 2)]

    def softmax_step(ci, lg_pair, carry):
        m, l = carry
        bias = jnp.where(sc_ref[ci] >= tile(t), 0.0, NEG_BIG)
        lg = [lg_pair[h // 2][:, (h % 2) * LANES:(h % 2 + 1) * LANES] + bias for h in heads]
        mx = [jnp.max(lg[h].reshape(nv, SUBLANES, LANES), axis=0) for h in heads]
        m_new = [jnp.maximum(m[h], _sublane_all(jnp.max, mx[h])) for h in heads]
        alpha = [jnp.exp2(m[h] - m_new[h]) for h in heads]
        p = [jnp.exp2(lg[h] - tile(m_new[h])) for h in heads]
        l_new = [alpha[h] * l[h] + jnp.sum(p[h].reshape(nv, SUBLANES, LANES), axis=0) for h in heads]
        return (m_new, l_new), alpha, [p[h].astype(BF16) for h in heads]

    def accumulate(ci, alpha, pb):
        pv = [functools.reduce(jnp.add, [
            jnp.dot(vt_ref[ci * (kc // vc) + j, h * D_HEAD:(h + 1) * D_HEAD, :], pb[h][j * vc:(j + 1) * vc],
                    preferred_element_type=F32) for j in range(kc // vc)]) for h in heads]
        for h in heads:
            acc_ref[h] = acc_ref[h] * jnp.broadcast_to(alpha[h][0:1], (D_HEAD, LANES)) + pv[h]

    def att_body(pi, carry):
        ca, cb = 2 * pi, 2 * pi + 1
        lg_a = logits(ca)
        lg_b = logits(cb)
        carry, alpha_a, pb_a = softmax_step(ca, lg_a, carry)
        accumulate(ca, alpha_a, pb_a)
        carry, alpha_b, pb_b = softmax_step(cb, lg_b, carry)
        accumulate(cb, alpha_b, pb_b)
        return carry

    init = ([jnp.full((SUBLANES, LANES), NEG_BIG, F32)] * H_ATT, [jnp.zeros((SUBLANES, LANES), F32)] * H_ATT)
    _, l = lax.fori_loop(0, nch // 2, att_body, init)
    for h in range(H_ATT):
        denom = jnp.broadcast_to(jnp.sum(l[h], axis=0, keepdims=True), (D_HEAD, LANES))
        o_ref[:, h * D_HEAD:(h + 1) * D_HEAD] = (acc_ref[h] / denom).T


def _dsa_prompt(aqt, qit, iwt, kbf, kib, vt, *, nsel):
    t = kbf.shape[0]
    vc = vt.shape[2]
    kc = PROMPT_KEY_CHUNK
    nchunks = t // kc
    qb = Q_BLOCK
    assert t % qb == 0 and t % (2 * kc) == 0 and kc % vc == 0 and vt.shape[0] * vc == t
    col = lambda h: pl.BlockSpec((h, qb), lambda i: (0, i))
    vmem = kbf.size * 2 * 2 + t * LANES * 2 + nchunks * kc * LANES * 4 + (12 << 20)
    return pl.pallas_call(
        functools.partial(_dsa_prompt_kernel, nsel=nsel, kc=kc),
        grid=(t // qb,),
        in_specs=[col(SEG), col(SEG), col(H_IDX), _resident(kbf.shape), _resident(kib.shape), _resident(vt.shape)],
        out_specs=pl.BlockSpec((qb, SEG), lambda i: (i, 0)),
        out_shape=jax.ShapeDtypeStruct((t, SEG), F32),
        scratch_shapes=[pltpu.VMEM((nchunks, kc, LANES), F32), pltpu.VMEM((H_ATT, D_HEAD, LANES), F32)],
        compiler_params=_vmem_params(vmem, ("arbitrary",)),
        name="dsa_prompt",
    )(aqt, qit, iwt, kbf, kib, vt)


def _page_group_copies(hbm_refs, bufs, sems, pt_ref, seq, group, slot, pages_per_group):
    copies = []
    for hbm, buf, sem in zip(hbm_refs, bufs, sems):
        for j in range(pages_per_group):
            page = pt_ref[seq, group * pages_per_group + j]
            copies.append(pltpu.make_async_copy(hbm.at[page], buf.at[slot, j], sem.at[slot]))
    return copies


def _stream_page_groups(hbm_refs, bufs, sems, pt_ref, n_groups, pages_per_group, consume, carry):
    assert n_groups % 2 == 0
    b = pl.program_id(0)
    n_seq = pl.num_programs(0)
    copies = functools.partial(_page_group_copies, hbm_refs, bufs, sems, pt_ref,
                               pages_per_group=pages_per_group)

    @pl.when(b == 0)
    def _():
        for c in copies(0, 0, 0):
            c.start()

    def pair_body(i, carry):
        for slot in range(2):
            g = 2 * i + slot

            @pl.when(g + 1 < n_groups)
            def _():
                for c in copies(b, g + 1, 1 - slot):
                    c.start()

            @pl.when(jnp.logical_and(g + 1 == n_groups, b + 1 < n_seq))
            def _():
                for c in copies(b + 1, 0, 1 - slot):
                    c.start()

            for c in copies(b, g, slot):
                c.wait()
            carry = consume(g, slot, carry)
        return carry

    return lax.fori_loop(0, n_groups // 2, pair_body, carry)


def _sample_scores_kernel(pt_ref, kidx_hbm, qi_ref, wi_ref, kin_ref, sc_ref, thr_ref, kbuf, sem,
                          *, n_pages, nsel, t_valid):
    rows = SAMPLE_ROWS
    gp, dp = SCORE_PAGES_PER_GROUP, SCORE_PAGES_PER_DOT
    qi = qi_ref[0]
    qhm = jnp.concatenate([qi[:, h * D_IDX:(h + 1) * D_IDX] for h in range(H_IDX)], axis=0).astype(BF16)
    wi = wi_ref[0] * IDX_SCALE

    def head_sum(s):
        acc = jnp.zeros((rows, s.shape[1]), F32)
        for h in range(H_IDX):
            acc = acc + jnp.maximum(s[h * rows:(h + 1) * rows], 0.0) * wi[:, h:h + 1]
        return acc

    def consume(g, slot, carry):
        for d in range(gp // dp):
            kt = jnp.concatenate([kbuf[slot, d * dp + j] for j in range(dp)], axis=1).astype(BF16)
            acc = head_sum(jnp.dot(qhm, kt, preferred_element_type=F32))
            for j in range(dp):
                sc_ref[0, g * gp + d * dp + j] = acc[:, j * LANES:(j + 1) * LANES]
        return carry

    _stream_page_groups([kidx_hbm], [kbuf], [sem], pt_ref, n_pages // gp, gp, consume, 0)

    kn = _pad_rows(kin_ref[0], LANES).astype(BF16)
    r = lax.broadcasted_iota(I32, (rows, LANES), 0)
    c = lax.broadcasted_iota(I32, (rows, LANES), 1)
    acc = head_sum(lax.dot_general(qhm, kn, NT_DIMS, preferred_element_type=F32))
    sc_ref[0, n_pages] = jnp.where(jnp.logical_and(c <= r, c < t_valid), acc, jnp.nan)
    thr_ref[0] = _topk_threshold_rows(sc_ref.at[0], n_pages + 1, nsel, rows, t_valid)


def _sample_scores(page_table, kidx_t, qi, wi, ki_new, *, nsel, t_valid):
    b, n_pages = page_table.shape
    _, _, page = kidx_t.shape
    gp = SCORE_PAGES_PER_GROUP
    assert page == LANES and n_pages % (2 * gp) == 0
    rows = SAMPLE_ROWS
    seq = lambda *tail: pl.BlockSpec((1,) + tail, lambda i, pt: (i,) + (0,) * len(tail))
    grid_spec = pltpu.PrefetchScalarGridSpec(
        num_scalar_prefetch=1,
        grid=(b,),
        in_specs=[pl.BlockSpec(memory_space=pl.ANY), seq(rows, SEG), seq(rows, H_IDX), seq(rows, D_IDX)],
        out_specs=[seq(n_pages + 1, rows, page), seq(rows, LANES)],
        scratch_shapes=[pltpu.VMEM((2, gp, D_IDX, page), F32), pltpu.SemaphoreType.DMA((2,))],
    )
    return pl.pallas_call(
        functools.partial(_sample_scores_kernel, n_pages=n_pages, nsel=nsel, t_valid=t_valid),
        grid_spec=grid_spec,
        out_shape=[jax.ShapeDtypeStruct((b, n_pages + 1, rows, page), F32),
                   jax.ShapeDtypeStruct((b, rows, LANES), F32)],
        compiler_params=_vmem_params(32 << 20, ("arbitrary",)),
        name="sample_scores",
    )(page_table, kidx_t, qi, wi, ki_new)


def _sample_attn_kernel(pt_ref, k_hbm, v_hbm, aq_ref, kn_ref, vn_ref, sc_ref, thr_ref, o_ref,
                        kbuf, vbuf, ksem, vsem, *, n_pages):
    rows = SAMPLE_ROWS
    gp = ATTN_PAGES_PER_GROUP
    page = LANES
    thr = thr_ref[0][:, 0:1]
    aq = aq_ref[0]
    qh = [aq[:, h * D_HEAD:(h + 1) * D_HEAD].astype(BF16) for h in range(H_ATT)]

    def head_rows(buf, slot, j, h):
        return buf[slot, j, pl.ds(h, page, stride=H_ATT), :].astype(BF16)

    def attend(carry, scores, keys, values):
        m, l, acc = carry
        sel = scores >= thr
        heads = range(H_ATT)
        lg = [lax.dot_general(qh[h], keys[h], NT_DIMS, preferred_element_type=F32) for h in heads]
        lg = [jnp.where(sel, lg[h] * ATT_SCALE, NEG_BIG) for h in heads]
        m_new = [jnp.maximum(m[h], jnp.max(lg[h], axis=1, keepdims=True)) for h in heads]
        alpha = [jnp.exp(m[h] - m_new[h]) for h in heads]
        p = [jnp.where(sel, jnp.exp(lg[h] - m_new[h]), 0.0) for h in heads]
        pv = [jnp.dot(p[h].astype(BF16), values[h], preferred_element_type=F32) for h in heads]
        return (m_new, [alpha[h] * l[h] + jnp.sum(p[h], axis=1, keepdims=True) for h in heads],
                [alpha[h] * acc[h] + pv[h] for h in heads])

    def consume(g, slot, carry):
        scores = jnp.concatenate([sc_ref[0, g * gp + j] for j in range(gp)], axis=1)
        keys = [jnp.concatenate([head_rows(kbuf, slot, j, h) for j in range(gp)], axis=0) for h in range(H_ATT)]
        values = [jnp.concatenate([head_rows(vbuf, slot, j, h) for j in range(gp)], axis=0) for h in range(H_ATT)]
        return attend(carry, scores, keys, values)

    init = ([jnp.full((rows, 1), NEG_BIG, F32)] * H_ATT, [jnp.zeros((rows, 1), F32)] * H_ATT,
            [jnp.zeros((rows, LANES), F32)] * H_ATT)
    carry = _stream_page_groups([k_hbm, v_hbm], [kbuf, vbuf], [ksem, vsem], pt_ref, n_pages // gp, gp,
                                consume, init)
    kn, vn = kn_ref[0], vn_ref[0]
    hs = lambda h: slice(h * D_HEAD, (h + 1) * D_HEAD)
    _, l, acc = attend(carry, sc_ref[0, n_pages],
                       [_pad_rows(kn[:, hs(h)], page).astype(BF16) for h in range(H_ATT)],
                       [_pad_rows(vn[:, hs(h)], page).astype(BF16) for h in range(H_ATT)])
    for h in range(H_ATT):
        o_ref[0, :, hs(h)] = acc[h] / l[h]


def _sample_attn(page_table, cache_k, cache_v, aq, k_new, v_new, scores, thr):
    b, n_pages = page_table.shape
    gp = ATTN_PAGES_PER_GROUP
    rows = SAMPLE_ROWS
    page_rows = cache_k.shape[1]
    assert page_rows == LANES * H_ATT and n_pages % (2 * gp) == 0
    seq = lambda *tail: pl.BlockSpec((1,) + tail, lambda i, pt: (i,) + (0,) * len(tail))
    buf = pltpu.VMEM((2, gp, page_rows, D_HEAD), F32)
    grid_spec = pltpu.PrefetchScalarGridSpec(
        num_scalar_prefetch=1,
        grid=(b,),
        in_specs=[pl.BlockSpec(memory_space=pl.ANY), pl.BlockSpec(memory_space=pl.ANY),
                  seq(rows, SEG), seq(rows, SEG), seq(rows, SEG), seq(n_pages + 1, rows, LANES), seq(rows, LANES)],
        out_specs=seq(rows, SEG),
        scratch_shapes=[buf, buf, pltpu.SemaphoreType.DMA((2,)), pltpu.SemaphoreType.DMA((2,))],
    )
    vmem = 2 * 2 * gp * page_rows * D_HEAD * 4 + (16 << 20)
    return pl.pallas_call(
        functools.partial(_sample_attn_kernel, n_pages=n_pages),
        grid_spec=grid_spec,
        out_shape=jax.ShapeDtypeStruct((b, rows, SEG), F32),
        compiler_params=_vmem_params(vmem, ("arbitrary",)),
        name="sample_attn",
    )(page_table, cache_k, cache_v, aq, k_new, v_new, scores, thr)


def _merge_kernel(x_ref, oh_ref, gh_ref, oa_ref, ga_ref, gnh_ref, gna_ref, wo_ref, lng_ref, lnb_ref, y_ref):
    def branch(o_ref, gate_ref, gain_ref):
        parts = []
        for h in range(H_ATT):
            o = o_ref[:, h * D_HEAD:(h + 1) * D_HEAD]
            parts.append(o * lax.rsqrt(jnp.mean(o * o, axis=-1, keepdims=True) + EPS))
        g = gate_ref[...]
        return (jnp.concatenate(parts, axis=1) * gain_ref[...]) * (g * _sigmoid(g))

    hh = branch(oh_ref, gh_ref, gnh_ref).astype(BF16)
    aa = branch(oa_ref, ga_ref, gna_ref).astype(BF16)
    mix = (jnp.dot(hh, wo_ref[:SEG, :], preferred_element_type=F32)
           + jnp.dot(aa, wo_ref[SEG:, :], preferred_element_type=F32))
    r = ALPHA * x_ref[...] + mix
    mu = jnp.mean(r, axis=-1, keepdims=True)
    rc = r - mu
    var = jnp.mean(rc * rc, axis=-1, keepdims=True)
    y_ref[...] = rc * lax.rsqrt(var + EPS) * lng_ref[...] + lnb_ref[...]


def _merge(x2, oh, gh, oa, ga, gnh, gna, wo, lng, lnb, *, tm):
    t, d = x2.shape
    assert t % tm == 0
    row = lambda w: pl.BlockSpec((tm, w), lambda i: (i, 0))
    return pl.pallas_call(
        _merge_kernel,
        grid=(t // tm,),
        in_specs=[row(d), row(SEG), row(SEG), row(SEG), row(SEG), _resident(gnh.shape), _resident(gna.shape),
                  _resident(wo.shape), _resident(lng.shape), _resident(lnb.shape)],
        out_specs=row(d),
        out_shape=jax.ShapeDtypeStruct((t, d), F32),
        compiler_params=_vmem_params(40 << 20, ("parallel",)),
        name="merge",
    )(x2, oh, gh, oa, ga, gnh, gna, wo, lng, lnb)


def kernel(x_prompt, x_sample, cache_k, cache_v, cache_kidx, state_hgrn, page_table,
           w_in, w_out, lb_logits, g_hgrn, g_attn, ln_g, ln_b):
    bp, t, d = x_prompt.shape
    bs, ts, _ = x_sample.shape
    n_pages = page_table.shape[1]
    page = cache_k.shape[2]
    assert w_in.shape[0] == DEPTH and bp == 1 and ts <= SAMPLE_ROWS
    main = N_SEG * SEG

    w = w_in[0]
    wm = w[:, :main].astype(BF16)
    wt = jnp.pad(w[:, main:], ((0, 0), (0, LANES - (D_IDX + H_IDX)))).astype(BF16)
    seg_cols = lambda j: w[:, j * SEG:(j + 1) * SEG]
    wx = jnp.concatenate([seg_cols(4), seg_cols(8), seg_cols(6), w[:, main + D_IDX:]], axis=1).T.astype(BF16)
    wo = w_out[0].astype(BF16)
    gnh, gna = g_hgrn[0][None, :], g_attn[0][None, :]
    lng, lnb = ln_g[0][None, :], ln_b[0][None, :]

    xp = x_prompt.reshape(t, d)
    kc = 2 * LANES
    (q, kk, v, lf, gh, ak, av, ag, ik, kbf, kib, aqt, qit, iwt, vt) = _project(
        xp, wm, wt, wx, lb_logits, tm=256, kc=kc)
    rs = lambda a: a.reshape(1, t, SEG)
    s0 = jnp.zeros((1, H_HGRN, D_HEAD, D_HEAD), F32)
    o_h, s_p = _hgrn(rs(q), rs(kk), rs(v), rs(lf), s0, rows=256, t_valid=HGRN_CHUNK)
    o_a = _dsa_prompt(aqt, qit, iwt, kbf, kib, vt, nsel=min(TOPK_MAX, t // 4))
    y_p = _merge(xp, o_h.reshape(t, SEG), gh, o_a, ag, gnh, gna, wo, lng, lnb, tm=512)

    rows = SAMPLE_ROWS
    xs = jnp.pad(x_sample, ((0, 0), (0, rows - ts), (0, 0))).reshape(bs * rows, d)
    (q, kk, v, lf, gh, ak_s, av_s, ag, ik_s, aq, iq, iw) = _project(
        xs, wm, wt, wx, lb_logits, tm=bs * rows, kc=0)
    rs = lambda a: a.reshape(bs, rows, a.shape[-1])
    o_h, s_s = _hgrn(rs(q), rs(kk), rs(v), rs(lf), state_hgrn[0], rows=rows, t_valid=ts)
    nsel = min(TOPK_MAX, (n_pages * page + ts) // 4)
    kidx_t = jnp.swapaxes(cache_kidx[0], 1, 2)
    scores, thr = _sample_scores(page_table, kidx_t, rs(iq), rs(iw), rs(ik_s), nsel=nsel, t_valid=ts)
    pages = lambda c: c[0].reshape(c.shape[1], page * H_ATT, D_HEAD)
    o_a = _sample_attn(page_table, pages(cache_k), pages(cache_v), rs(aq), rs(ak_s), rs(av_s), scores, thr)
    y_s = _merge(xs, o_h.reshape(bs * rows, SEG), gh, o_a.reshape(bs * rows, SEG), ag,
                 gnh, gna, wo, lng, lnb, tm=bs * rows)

    heads = lambda a, n: a.reshape(1, 1, n, H_ATT, D_HEAD)
    sample = lambda a: a.reshape(bs, rows, -1)[:, :ts]
    return (y_p.reshape(1, t, d),
            sample(y_s),
            heads(ak, t), heads(av, t), ik.reshape(1, 1, t, D_IDX), s_p[None],
            sample(ak_s).reshape(1, bs, ts, H_ATT, D_HEAD), sample(av_s).reshape(1, bs, ts, H_ATT, D_HEAD),
            sample(ik_s)[None], s_s[None])
```

```python
import functools

import jax
import jax.numpy as jnp
from jax import lax
from jax.experimental import pallas as pl
from jax.experimental.pallas import tpu as pltpu

F32 = jnp.float32
BF16 = jnp.bfloat16
I32 = jnp.int32

H_HGRN = 4
H_ATT = 4
D_HEAD = 128
H_IDX = 8
D_IDX = 64
SEG = 512
N_SEG = 9
TOPK_MAX = 256
HGRN_CHUNK = 64
Q_BLOCK = 256
PROMPT_KEY_CHUNK = 512
EPS = 1e-5
DEPTH = 1
ALPHA = (2 * DEPTH) ** 0.25
IDX_SCALE = H_IDX ** -0.5 * D_IDX ** -0.5
ATT_SCALE = D_HEAD ** -0.5
LOG2E = 1.4426950408889634

LANES = 128
SUBLANES = 8
SAMPLE_ROWS = SUBLANES

INT_MAX = 2 ** 31 - 1
KEY_NEG_INF = -2139095041
KEY_POS_INF = 2139095040
ZERO_KEY_MIN = -(2 ** 23)
ZERO_KEY_MAX = 2 ** 23 - 1
NEG_BIG = -1e30
MAX_BISECT_STEPS = 34
TIE_GROUP_KEYS = 1

SCORE_PAGES_PER_GROUP = 32
SCORE_PAGES_PER_DOT = 8
ATTN_PAGES_PER_GROUP = 8

NT_DIMS = (((1,), (1,)), ((), ()))
TN_DIMS = (((0,), (0,)), ((), ()))


def _sigmoid(x):
    return 1.0 / (1.0 + jnp.exp(-x))


def _vmem_params(nbytes, semantics):
    return pltpu.CompilerParams(dimension_semantics=semantics, vmem_limit_bytes=int(nbytes))


def _resident(shape):
    zeros = (0,) * len(shape)
    return pl.BlockSpec(shape, lambda *_: zeros, pipeline_mode=pl.Buffered(1))


def _lane_sum(x):
    return jnp.broadcast_to(jnp.sum(x, axis=1, keepdims=True), x.shape)


def _sublane_all(op, x):
    return jnp.broadcast_to(op(x, axis=0, keepdims=True), x.shape)


def _pad_rows(a, n):
    return jnp.concatenate([a, jnp.zeros((n - a.shape[0], a.shape[1]), a.dtype)], axis=0)


def _proj_kernel(x_ref, wm_ref, wt_ref, wx_ref, lbl_ref, *out_refs, kc):
    x = x_ref[...].astype(BF16)

    def seg(j):
        return jnp.dot(x, wm_ref[:, j * SEG:(j + 1) * SEG], preferred_element_type=F32)

    q_ref, kk_ref, v_ref, lf_ref, gh_ref, ak_ref, av_ref, ag_ref, ik_ref = out_refs[:9]
    hq = seg(0)
    q_ref[...] = hq * _sigmoid(hq)
    lbl = lbl_ref[...]
    e = jnp.exp(lbl - jnp.max(lbl, axis=0, keepdims=True))
    lb = e[0:1] / jnp.sum(e, axis=0, keepdims=True)
    f = lb + (1.0 - lb) * _sigmoid(seg(1))
    kk_ref[...] = 1.0 - f
    lf_ref[...] = jnp.log(f)
    v_ref[...] = seg(2)
    gh_ref[...] = seg(3)
    ak = seg(5)
    ak_ref[...] = ak
    av_ref[...] = seg(6)
    ag_ref[...] = seg(7)
    tail = jnp.dot(x, wt_ref[...], preferred_element_type=F32)
    ik = tail[:, :D_IDX]
    ik_ref[...] = ik
    if kc == 0:
        aq_ref, iq_ref, iw_ref = out_refs[9:]
        aq_ref[...] = seg(4)
        iq_ref[...] = seg(8)
        iw_ref[...] = tail[:, D_IDX:D_IDX + H_IDX]
    else:
        kbf_ref, kib_ref, aqt_ref, qit_ref, iwt_ref, vt_ref = out_refs[9:]
        kbf_ref[...] = ak.astype(BF16)
        kib_ref[...] = ik.astype(BF16)
        xt = lax.dot_general(wx_ref[...], x, NT_DIMS, preferred_element_type=F32)
        aqt_ref[...] = (xt[:SEG] * (ATT_SCALE * LOG2E)).astype(BF16)
        qit_ref[...] = xt[SEG:2 * SEG].astype(BF16)
        iwt_ref[...] = xt[3 * SEG:] * IDX_SCALE
        vt = xt[2 * SEG:3 * SEG].astype(BF16)
        for j in range(vt.shape[1] // kc):
            vt_ref[j] = vt[:, j * kc:(j + 1) * kc]


def _project(x2, wm, wt, wx, lbl, *, tm, kc):
    t, d = x2.shape
    assert t % tm == 0 and (kc == 0 or tm % kc == 0)
    row = lambda w: pl.BlockSpec((tm, w), lambda i: (i, 0))
    col = lambda h: pl.BlockSpec((h, tm), lambda i: (0, i))
    f32 = lambda *s: jax.ShapeDtypeStruct(s, F32)
    b16 = lambda *s: jax.ShapeDtypeStruct(s, BF16)
    out_shape = [f32(t, SEG)] * 8 + [f32(t, D_IDX)]
    out_specs = [row(SEG)] * 8 + [row(D_IDX)]
    if kc == 0:
        out_shape += [f32(t, SEG), f32(t, SEG), f32(t, H_IDX)]
        out_specs += [row(SEG), row(SEG), row(H_IDX)]
    else:
        out_shape += [b16(t, SEG), b16(t, D_IDX), b16(SEG, t), b16(SEG, t), f32(H_IDX, t), b16(t // kc, SEG, kc)]
        out_specs += [row(SEG), row(D_IDX), col(SEG), col(SEG), col(H_IDX),
                      pl.BlockSpec((tm // kc, SEG, kc), lambda i: (i, 0, 0))]
    vmem = (wm.size + wx.size) * 2 + 2 * tm * d * 4 + 2 * tm * SEG * 12 * 4 + (8 << 20)
    return pl.pallas_call(
        functools.partial(_proj_kernel, kc=kc),
        grid=(t // tm,),
        in_specs=[row(d), _resident(wm.shape), _resident(wt.shape), _resident(wx.shape), _resident(lbl.shape)],
        out_specs=out_specs,
        out_shape=out_shape,
        compiler_params=_vmem_params(vmem, ("parallel",)),
        name="proj",
    )(x2, wm, wt, wx, lbl)


def _hgrn_kernel(q_ref, kk_ref, v_ref, lf_ref, s0_ref, o_ref, sout_ref, st_ref, *, rows, t_valid):
    c_len = HGRN_CHUNK
    step = pl.program_id(1)

    @pl.when(step == 0)
    def _():
        for h in range(H_HGRN):
            st_ref[h] = s0_ref[0, h].T

    r = lax.broadcasted_iota(I32, (c_len, c_len), 0)
    c = lax.broadcasted_iota(I32, (c_len, c_len), 1)
    causal = r >= c
    tri = causal.astype(F32)
    pad = c_len - rows if rows < c_len else 0

    def load(ref, r0, n, h):
        a = ref[0, r0:r0 + n, h * D_HEAD:(h + 1) * D_HEAD]
        if pad:
            a = jnp.concatenate([a, jnp.zeros((pad, D_HEAD), F32)], axis=0)
        return a

    n_rows = min(rows, c_len)
    n_chunks = max(rows // c_len, 1)
    tiles = [(ci, h) for ci in range(n_chunks) for h in range(H_HGRN)]

    def log_decay(ci, h):
        lf = load(lf_ref, ci * c_len, n_rows, h)
        if t_valid < c_len:
            lf = jnp.where(lax.broadcasted_iota(I32, lf.shape, 0) < t_valid, lf, 0.0)
        return jnp.dot(tri, lf, precision=lax.Precision.HIGHEST, preferred_element_type=F32)

    b = {k: log_decay(*k) for k in tiles}
    b_end = {k: b[k][c_len - 1:c_len, :] for k in tiles}
    qd = {(ci, h): (load(q_ref, ci * c_len, n_rows, h) * jnp.exp(b[ci, h])).astype(BF16) for ci, h in tiles}
    kk = {(ci, h): load(kk_ref, ci * c_len, n_rows, h) for ci, h in tiles}
    kd = {k: (kk[k] * jnp.exp(-b[k])).astype(BF16) for k in tiles}
    k2 = {k: (kk[k] * jnp.exp(b_end[k] - b[k])).astype(BF16) for k in tiles}
    vb = {(ci, h): load(v_ref, ci * c_len, n_rows, h).astype(BF16) for ci, h in tiles}
    att = {k: lax.dot_general(qd[k], kd[k], NT_DIMS, preferred_element_type=F32) for k in tiles}
    att = {k: jnp.where(causal, att[k], 0.0).astype(BF16) for k in tiles}
    kv = {k: lax.dot_general(vb[k], k2[k], TN_DIMS, preferred_element_type=F32) for k in tiles}
    o_intra = {k: jnp.dot(att[k], vb[k], preferred_element_type=F32) for k in tiles}
    pre = {k: (qd[k], o_intra[k], kv[k], jnp.exp(b_end[k])) for k in tiles}
    for h in range(H_HGRN):
        st = st_ref[h]
        for ci in range(n_chunks):
            qd, o_intra, kv, decay = pre[ci, h]
            o = o_intra + lax.dot_general(qd, st.astype(BF16), NT_DIMS, preferred_element_type=F32)
            o_ref[0, ci * c_len:ci * c_len + n_rows, h * D_HEAD:(h + 1) * D_HEAD] = o[:n_rows]
            st = st * decay + kv
        st_ref[h] = st

    @pl.when(step == pl.num_programs(1) - 1)
    def _():
        for h in range(H_HGRN):
            sout_ref[0, h] = st_ref[h].T


def _hgrn(q, kk, v, lf, s0, *, rows, t_valid):
    b, t, _ = q.shape
    assert t % rows == 0
    blk = pl.BlockSpec((1, rows, SEG), lambda i, j: (i, j, 0))
    sblk = pl.BlockSpec((1, H_HGRN, D_HEAD, D_HEAD), lambda i, j: (i, 0, 0, 0))
    return pl.pallas_call(
        functools.partial(_hgrn_kernel, rows=rows, t_valid=t_valid),
        grid=(b, t // rows),
        in_specs=[blk, blk, blk, blk, sblk],
        out_specs=[blk, sblk],
        out_shape=[jax.ShapeDtypeStruct((b, t, SEG), F32),
                   jax.ShapeDtypeStruct((b, H_HGRN, D_HEAD, D_HEAD), F32)],
        scratch_shapes=[pltpu.VMEM((H_HGRN, D_HEAD, D_HEAD), F32)],
        compiler_params=_vmem_params(32 << 20, ("parallel", "arbitrary")),
        name="hgrn",
    )(q, kk, v, lf, s0)


def _is_zero_key(k):
    return jnp.logical_and(k >= ZERO_KEY_MIN, k <= ZERO_KEY_MAX)


def _key_to_float(k):
    f = lax.bitcast_convert_type(k ^ (lax.shift_right_arithmetic(k, 31) & INT_MAX), F32)
    return jnp.where(_is_zero_key(k), 0.0, f)


def _next_key(k, step=1):
    return jnp.where(_is_zero_key(k), jnp.maximum(k + step, ZERO_KEY_MAX + 1), k + step)


def _tie_group(lo):
    g = TIE_GROUP_KEYS
    start = lo if g == 1 else jnp.where(lo > KEY_NEG_INF + g, lo & (-g), lo)
    return start, _next_key(jnp.minimum(start, KEY_POS_INF + 1 - g), g)


def _bisect_threshold(count_ge, nsel, lo0, hi0):
    def active(lo, hi):
        return (lo + 1) != hi

    def cond(state):
        it, lo, hi, _ = state
        return jnp.logical_and(it < MAX_BISECT_STEPS,
                               jnp.max(jnp.where(active(lo, hi), 1.0, 0.0)) > 0.0)

    def body(state):
        it, lo, hi, clo = state
        mid = (lo & hi) + lax.shift_right_arithmetic(lo ^ hi, 1)
        cnt = count_ge(_key_to_float(mid))
        ge = cnt >= nsel
        exact = cnt == nsel
        act = active(lo, hi)
        new_hi = jnp.where(exact, mid + 1, jnp.where(ge, hi, mid))
        return (it + 1,
                jnp.where(jnp.logical_and(act, ge), mid, lo),
                jnp.where(act, new_hi, hi),
                jnp.where(jnp.logical_and(act, ge), cnt, clo))

    _, lo, _, clo = lax.while_loop(cond, body, (jnp.int32(0), lo0, hi0, jnp.zeros(lo0.shape, F32)))
    return lo, clo


def _topk_threshold_rows(sc_ref, nch, nsel, rows, rows_valid):
    shape = (rows, LANES)

    def count_ge(thr):
        parts = [jnp.zeros(shape, F32)] * 4
        for ci in range(nch):
            parts[ci % 4] = parts[ci % 4] + jnp.where(sc_ref[ci] >= thr, 1.0, 0.0)
        return _lane_sum((parts[0] + parts[1]) + (parts[2] + parts[3]))

    real = lax.broadcasted_iota(I32, shape, 0) < rows_valid
    lo, clo = _bisect_threshold(count_ge, nsel, jnp.full(shape, KEY_NEG_INF, I32),
                                jnp.where(real, KEY_POS_INF + 1, KEY_NEG_INF + 1))
    lo, group_end = _tie_group(lo)
    if TIE_GROUP_KEYS > 1:
        clo = jnp.where(real, count_ge(_key_to_float(lo)), 0.0)
    t = _key_to_float(lo)

    @pl.when(jnp.max(clo) > nsel)
    def _():
        t_next = _key_to_float(group_end)
        need = nsel - count_ge(t_next)
        zero = jnp.zeros(shape, F32)

        def residual(s):
            return jnp.where(jnp.logical_and(s >= t, s < t_next), s - t, jnp.nan)

        def count_res(pred):
            return _lane_sum(functools.reduce(
                jnp.add, [jnp.where(pred(residual(sc_ref[ci])), 1.0, 0.0) for ci in range(nch)]))

        key0 = jnp.zeros(shape, I32)
        lo2, _ = lax.cond(jnp.max(count_res(lambda r: r > 0.0)) > 0.0,
                          lambda: _bisect_threshold(lambda thr: count_res(lambda r: r >= thr), need, key0,
                                                    jnp.full(shape, KEY_POS_INF + 1, I32)),
                          lambda: (key0, zero))
        t2 = _key_to_float(lo2)
        t2_next = _key_to_float(_next_key(lo2))
        need2 = need - count_res(lambda r: r >= t2_next)
        strict_upper = (lax.broadcasted_iota(I32, (LANES, LANES), 0)
                        < lax.broadcasted_iota(I32, (LANES, LANES), 1)).astype(BF16)

        def tie_body(ci, seen):
            s = sc_ref[ci]
            res = residual(s)
            tied = jnp.logical_and(res >= t2, res < t2_next)
            tiedf = jnp.where(tied, 1.0, 0.0)
            before = seen + jnp.dot(tiedf.astype(BF16), strict_upper, preferred_element_type=F32)
            lose = jnp.logical_or(res < t2, jnp.logical_and(tied, before >= need2))
            sc_ref[ci] = jnp.where(lose, jnp.nan, s)
            return seen + _lane_sum(tiedf)

        lax.fori_loop(0, nch, tie_body, zero)

    return t


def _topk_threshold_cols(sc_ref, nch, nsel, kc):
    ql = sc_ref.shape[2]
    shape = (SUBLANES, ql)
    nv = kc // SUBLANES
    zero = jnp.zeros(shape, F32)

    def count_ge(thr):
        def body(ci, parts):
            parts = list(parts)
            for v in range(nv):
                s = sc_ref[ci, v * SUBLANES:(v + 1) * SUBLANES, :]
                parts[v % 4] = parts[v % 4] + jnp.where(s >= thr, 1.0, 0.0)
            return tuple(parts)
        parts = lax.fori_loop(0, nch, body, (zero,) * 4)
        return _sublane_all(jnp.sum, (parts[0] + parts[1]) + (parts[2] + parts[3]))

    def reduce_tiles(fn, op, init):
        def body(ci, parts):
            parts = list(parts)
            for v in range(nv):
                s = sc_ref[ci, v * SUBLANES:(v + 1) * SUBLANES, :]
                parts[v % 4] = op(parts[v % 4], fn(s))
            return tuple(parts)
        parts = lax.fori_loop(0, nch, body, (init,) * 4)
        return op(op(parts[0], parts[1]), op(parts[2], parts[3]))

    lo, clo = _bisect_threshold(count_ge, nsel, jnp.full(shape, KEY_NEG_INF, I32),
                                jnp.full(shape, KEY_POS_INF + 1, I32))
    lo, group_end = _tie_group(lo)
    if TIE_GROUP_KEYS > 1:
        clo = count_ge(_key_to_float(lo))
    t = _key_to_float(lo)

    @pl.when(jnp.max(clo) > nsel)
    def _():
        t_next = _key_to_float(group_end)
        need = nsel - count_ge(t_next)

        def residual(s, t_, t_next_):
            return jnp.where(jnp.logical_and(s >= t_, s < t_next_), s - t_, jnp.nan)

        def count_res_ge(thr):
            return _sublane_all(jnp.sum, reduce_tiles(
                lambda s: jnp.where(residual(s, t, t_next) >= thr, 1.0, 0.0), jnp.add, zero))

        rmax = _sublane_all(jnp.max, reduce_tiles(
            lambda s: jnp.where(residual(s, t, t_next) > 0.0, 1.0, 0.0), jnp.maximum, zero))
        key0 = jnp.zeros(shape, I32)
        lo2, _ = lax.cond(jnp.max(rmax) > 0.0,
                          lambda: _bisect_threshold(count_res_ge, need, key0, jnp.full(shape, KEY_POS_INF + 1, I32)),
                          lambda: (key0, zero))
        t2 = _key_to_float(lo2)
        t2_next = _key_to_float(_next_key(lo2))
        need2 = (need - count_res_ge(t2_next))[0:1]
        t_r, tn_r, t2_r, t2n_r = t[0:1], t_next[0:1], t2[0:1], t2_next[0:1]
        strict_lower = (lax.broadcasted_iota(I32, (kc, kc), 1)
                        < lax.broadcasted_iota(I32, (kc, kc), 0)).astype(BF16)

        def tie_body(ci, seen):
            s = sc_ref[ci]
            res = residual(s, t_r, tn_r)
            tied = jnp.logical_and(res >= t2_r, res < t2n_r)
            tiedf = jnp.where(tied, 1.0, 0.0)
            before = seen + jnp.dot(strict_lower, tiedf.astype(BF16), preferred_element_type=F32)
            lose = jnp.logical_or(res < t2_r, jnp.logical_and(tied, before >= need2))
            sc_ref[ci] = jnp.where(lose, jnp.nan, s)
            return seen + jnp.sum(tiedf, axis=0, keepdims=True)

        lax.fori_loop(0, nch, tie_body, jnp.zeros((1, ql), F32))

    return t


def _dsa_prompt_kernel(aqt_ref, qit_ref, iwt_ref, kbf_ref, kib_ref, vt_ref, o_ref, sc_ref, acc_ref, *, nsel, kc):
    qb = ql = o_ref.shape[0]
    vc = vt_ref.shape[2]
    nv = kc // SUBLANES
    i = pl.program_id(0)
    nch = 2 * (((i + 1) * qb + 2 * kc - 1) // (2 * kc))
    n_full = (i * qb + 1) // kc

    qit = qit_ref[...]
    rhs = jnp.concatenate([qit[h * D_IDX:(h + 1) * D_IDX, :] for h in range(H_IDX)], axis=1)
    w = iwt_ref[...]

    def scores(ci):
        parts = []
        for j in range(kc // vc):
            r0 = pl.multiple_of(ci * kc + j * vc, vc)
            s = jnp.dot(kib_ref[pl.ds(r0, vc), :], rhs, preferred_element_type=F32)
            acc = jnp.zeros((vc, ql), F32)
            for h in range(H_IDX):
                acc = acc + jnp.maximum(s[:, h * ql:(h + 1) * ql], 0.0) * w[h:h + 1, :]
            parts.append(acc)
        return jnp.concatenate(parts, axis=0)

    def score_body(ci, carry):
        sc_ref[ci] = scores(ci)
        return carry

    def masked_score_body(ci, carry):
        kpos = ci * kc + lax.broadcasted_iota(I32, (kc, ql), 0)
        qpos = i * qb + lax.broadcasted_iota(I32, (kc, ql), 1)
        sc_ref[ci] = jnp.where(kpos <= qpos, scores(ci), jnp.nan)
        return carry

    lax.fori_loop(0, n_full, score_body, 0)
    lax.fori_loop(n_full, nch, masked_score_body, 0)

    t = _topk_threshold_cols(sc_ref, nch, nsel, kc)

    aqt = aqt_ref[...]
    qh = [aqt[h * D_HEAD:(h + 1) * D_HEAD, :] for h in range(H_ATT)]
    zq = jnp.zeros((D_HEAD, ql), BF16)
    qpair = [jnp.concatenate([jnp.concatenate([qh[2 * p], zq], axis=1),
                              jnp.concatenate([zq, qh[2 * p + 1]], axis=1)], axis=0) for p in range(H_ATT // 2)]
    for h in range(H_ATT):
        acc_ref[h] = jnp.zeros((D_HEAD, ql), F32)

    def tile(x):
        return jnp.broadcast_to(x[None], (nv, SUBLANES, ql)).reshape(kc, ql)

    heads = range(H_ATT)

    def logits(ci):
        r0 = pl.multiple_of(ci * kc, kc)
        return [jnp.dot(kbf_ref[pl.ds(r0, kc), p * 2 * D_HEAD:(p + 1) * 2 * D_HEAD], qpair[p],
                        preferred_element_type=F32) for p in range(H_ATT // 2)]

    def softmax_step(ci, lg_pair, carry):
        m, l = carry
        bias = jnp.where(sc_ref[ci] >= tile(t), 0.0, NEG_BIG)
        lg = [lg_pair[h // 2][:, (h % 2) * ql:(h % 2 + 1) * ql] + bias for h in heads]
        mx = [jnp.max(lg[h].reshape(nv, SUBLANES, ql), axis=0) for h in heads]
        m_new = [jnp.maximum(m[h], _sublane_all(jnp.max, mx[h])) for h in heads]
        alpha = [jnp.exp2(m[h] - m_new[h]) for h in heads]
        p = [jnp.exp2(lg[h] - tile(m_new[h])) for h in heads]
        l_new = [alpha[h] * l[h] + jnp.sum(p[h].reshape(nv, SUBLANES, ql), axis=0) for h in heads]
        return (m_new, l_new), alpha, [p[h].astype(BF16) for h in heads]

    def accumulate(ci, alpha, pb):
        pv = [functools.reduce(jnp.add, [
            jnp.dot(vt_ref[ci * (kc // vc) + j, h * D_HEAD:(h + 1) * D_HEAD, :], pb[h][j * vc:(j + 1) * vc],
                    preferred_element_type=F32) for j in range(kc // vc)]) for h in heads]
        for h in heads:
            acc_ref[h] = acc_ref[h] * jnp.broadcast_to(alpha[h][0:1], (D_HEAD, ql)) + pv[h]

    def att_body(pi, carry):
        ca, cb = 2 * pi, 2 * pi + 1
        lg_a = logits(ca)
        lg_b = logits(cb)
        carry, alpha_a, pb_a = softmax_step(ca, lg_a, carry)
        accumulate(ca, alpha_a, pb_a)
        carry, alpha_b, pb_b = softmax_step(cb, lg_b, carry)
        accumulate(cb, alpha_b, pb_b)
        return carry

    init = ([jnp.full((SUBLANES, ql), NEG_BIG, F32)] * H_ATT, [jnp.zeros((SUBLANES, ql), F32)] * H_ATT)
    _, l = lax.fori_loop(0, nch // 2, att_body, init)
    for h in range(H_ATT):
        denom = jnp.broadcast_to(jnp.sum(l[h], axis=0, keepdims=True), (D_HEAD, ql))
        o_ref[:, h * D_HEAD:(h + 1) * D_HEAD] = (acc_ref[h] / denom).T


def _dsa_prompt(aqt, qit, iwt, kbf, kib, vt, *, nsel):
    t = kbf.shape[0]
    vc = vt.shape[2]
    kc = PROMPT_KEY_CHUNK
    nchunks = t // kc
    qb = Q_BLOCK
    assert t % qb == 0 and t % (2 * kc) == 0 and kc % vc == 0 and vt.shape[0] * vc == t
    col = lambda h: pl.BlockSpec((h, qb), lambda i: (0, i))
    vmem = kbf.size * 2 * 2 + t * LANES * 2 + nchunks * kc * qb * 4 + (6 << 20)
    return pl.pallas_call(
        functools.partial(_dsa_prompt_kernel, nsel=nsel, kc=kc),
        grid=(t // qb,),
        in_specs=[col(SEG), col(SEG), col(H_IDX), _resident(kbf.shape), _resident(kib.shape), _resident(vt.shape)],
        out_specs=pl.BlockSpec((qb, SEG), lambda i: (i, 0)),
        out_shape=jax.ShapeDtypeStruct((t, SEG), F32),
        scratch_shapes=[pltpu.VMEM((nchunks, kc, qb), F32), pltpu.VMEM((H_ATT, D_HEAD, qb), F32)],
        compiler_params=_vmem_params(vmem, ("arbitrary",)),
        name="dsa_prompt",
    )(aqt, qit, iwt, kbf, kib, vt)


def _page_group_copies(hbm_refs, bufs, sems, pt_ref, seq, group, slot, pages_per_group):
    copies = []
    for hbm, buf, sem in zip(hbm_refs, bufs, sems):
        for j in range(pages_per_group):
            page = pt_ref[seq, group * pages_per_group + j]
            copies.append(pltpu.make_async_copy(hbm.at[page], buf.at[slot, j], sem.at[slot]))
    return copies


def _stream_page_groups(hbm_refs, bufs, sems, pt_ref, n_groups, pages_per_group, consume, carry):
    assert n_groups % 2 == 0
    b = pl.program_id(0)
    n_seq = pl.num_programs(0)
    copies = functools.partial(_page_group_copies, hbm_refs, bufs, sems, pt_ref,
                               pages_per_group=pages_per_group)

    @pl.when(b == 0)
    def _():
        for c in copies(0, 0, 0):
            c.start()

    def pair_body(i, carry):
        for slot in range(2):
            g = 2 * i + slot

            @pl.when(g + 1 < n_groups)
            def _():
                for c in copies(b, g + 1, 1 - slot):
                    c.start()

            @pl.when(jnp.logical_and(g + 1 == n_groups, b + 1 < n_seq))
            def _():
                for c in copies(b + 1, 0, 1 - slot):
                    c.start()

            for c in copies(b, g, slot):
                c.wait()
            carry = consume(g, slot, carry)
        return carry

    return lax.fori_loop(0, n_groups // 2, pair_body, carry)


def _sample_scores_kernel(pt_ref, kidx_hbm, qi_ref, wi_ref, kin_ref, sc_ref, thr_ref, kbuf, sem,
                          *, n_pages, nsel, t_valid):
    rows = SAMPLE_ROWS
    gp, dp = SCORE_PAGES_PER_GROUP, SCORE_PAGES_PER_DOT
    qi = qi_ref[0]
    qhm = jnp.concatenate([qi[:, h * D_IDX:(h + 1) * D_IDX] for h in range(H_IDX)], axis=0).astype(BF16)
    wi = wi_ref[0] * IDX_SCALE

    def head_sum(s):
        acc = jnp.zeros((rows, s.shape[1]), F32)
        for h in range(H_IDX):
            acc = acc + jnp.maximum(s[h * rows:(h + 1) * rows], 0.0) * wi[:, h:h + 1]
        return acc

    def consume(g, slot, carry):
        for d in range(gp // dp):
            kt = jnp.concatenate([kbuf[slot, d * dp + j] for j in range(dp)], axis=1).astype(BF16)
            acc = head_sum(jnp.dot(qhm, kt, preferred_element_type=F32))
            for j in range(dp):
                sc_ref[0, g * gp + d * dp + j] = acc[:, j * LANES:(j + 1) * LANES]
        return carry

    _stream_page_groups([kidx_hbm], [kbuf], [sem], pt_ref, n_pages // gp, gp, consume, 0)

    kn = _pad_rows(kin_ref[0], LANES).astype(BF16)
    r = lax.broadcasted_iota(I32, (rows, LANES), 0)
    c = lax.broadcasted_iota(I32, (rows, LANES), 1)
    acc = head_sum(lax.dot_general(qhm, kn, NT_DIMS, preferred_element_type=F32))
    sc_ref[0, n_pages] = jnp.where(jnp.logical_and(c <= r, c < t_valid), acc, jnp.nan)
    thr_ref[0] = _topk_threshold_rows(sc_ref.at[0], n_pages + 1, nsel, rows, t_valid)


def _sample_scores(page_table, kidx_t, qi, wi, ki_new, *, nsel, t_valid):
    b, n_pages = page_table.shape
    _, _, page = kidx_t.shape
    gp = SCORE_PAGES_PER_GROUP
    assert page == LANES and n_pages % (2 * gp) == 0
    rows = SAMPLE_ROWS
    seq = lambda *tail: pl.BlockSpec((1,) + tail, lambda i, pt: (i,) + (0,) * len(tail))
    grid_spec = pltpu.PrefetchScalarGridSpec(
        num_scalar_prefetch=1,
        grid=(b,),
        in_specs=[pl.BlockSpec(memory_space=pl.ANY), seq(rows, SEG), seq(rows, H_IDX), seq(rows, D_IDX)],
        out_specs=[seq(n_pages + 1, rows, page), seq(rows, LANES)],
        scratch_shapes=[pltpu.VMEM((2, gp, D_IDX, page), F32), pltpu.SemaphoreType.DMA((2,))],
    )
    return pl.pallas_call(
        functools.partial(_sample_scores_kernel, n_pages=n_pages, nsel=nsel, t_valid=t_valid),
        grid_spec=grid_spec,
        out_shape=[jax.ShapeDtypeStruct((b, n_pages + 1, rows, page), F32),
                   jax.ShapeDtypeStruct((b, rows, LANES), F32)],
        compiler_params=_vmem_params(32 << 20, ("arbitrary",)),
        name="sample_scores",
    )(page_table, kidx_t, qi, wi, ki_new)


def _sample_attn_kernel(pt_ref, k_hbm, v_hbm, aq_ref, kn_ref, vn_ref, sc_ref, thr_ref, o_ref,
                        kbuf, vbuf, ksem, vsem, *, n_pages):
    rows = SAMPLE_ROWS
    gp = ATTN_PAGES_PER_GROUP
    page = LANES
    thr = thr_ref[0][:, 0:1]
    aq = aq_ref[0]
    qh = [aq[:, h * D_HEAD:(h + 1) * D_HEAD].astype(BF16) for h in range(H_ATT)]

    def head_rows(buf, slot, j, h):
        return buf[slot, j, pl.ds(h, page, stride=H_ATT), :].astype(BF16)

    def attend(carry, scores, keys, values):
        m, l, acc = carry
        sel = scores >= thr
        heads = range(H_ATT)
        lg = [lax.dot_general(qh[h], keys[h], NT_DIMS, preferred_element_type=F32) for h in heads]
        lg = [jnp.where(sel, lg[h] * ATT_SCALE, NEG_BIG) for h in heads]
        m_new = [jnp.maximum(m[h], jnp.max(lg[h], axis=1, keepdims=True)) for h in heads]
        alpha = [jnp.exp(m[h] - m_new[h]) for h in heads]
        p = [jnp.where(sel, jnp.exp(lg[h] - m_new[h]), 0.0) for h in heads]
        pv = [jnp.dot(p[h].astype(BF16), values[h], preferred_element_type=F32) for h in heads]
        return (m_new, [alpha[h] * l[h] + jnp.sum(p[h], axis=1, keepdims=True) for h in heads],
                [alpha[h] * acc[h] + pv[h] for h in heads])

    def consume(g, slot, carry):
        scores = jnp.concatenate([sc_ref[0, g * gp + j] for j in range(gp)], axis=1)
        keys = [jnp.concatenate([head_rows(kbuf, slot, j, h) for j in range(gp)], axis=0) for h in range(H_ATT)]
        values = [jnp.concatenate([head_rows(vbuf, slot, j, h) for j in range(gp)], axis=0) for h in range(H_ATT)]
        return attend(carry, scores, keys, values)

    init = ([jnp.full((rows, 1), NEG_BIG, F32)] * H_ATT, [jnp.zeros((rows, 1), F32)] * H_ATT,
            [jnp.zeros((rows, LANES), F32)] * H_ATT)
    carry = _stream_page_groups([k_hbm, v_hbm], [kbuf, vbuf], [ksem, vsem], pt_ref, n_pages // gp, gp,
                                consume, init)
    kn, vn = kn_ref[0], vn_ref[0]
    hs = lambda h: slice(h * D_HEAD, (h + 1) * D_HEAD)
    _, l, acc = attend(carry, sc_ref[0, n_pages],
                       [_pad_rows(kn[:, hs(h)], page).astype(BF16) for h in range(H_ATT)],
                       [_pad_rows(vn[:, hs(h)], page).astype(BF16) for h in range(H_ATT)])
    for h in range(H_ATT):
        o_ref[0, :, hs(h)] = acc[h] / l[h]


def _sample_attn(page_table, cache_k, cache_v, aq, k_new, v_new, scores, thr):
    b, n_pages = page_table.shape
    gp = ATTN_PAGES_PER_GROUP
    rows = SAMPLE_ROWS
    page_rows = cache_k.shape[1]
    assert page_rows == LANES * H_ATT and n_pages % (2 * gp) == 0
    seq = lambda *tail: pl.BlockSpec((1,) + tail, lambda i, pt: (i,) + (0,) * len(tail))
    buf = pltpu.VMEM((2, gp, page_rows, D_HEAD), F32)
    grid_spec = pltpu.PrefetchScalarGridSpec(
        num_scalar_prefetch=1,
        grid=(b,),
        in_specs=[pl.BlockSpec(memory_space=pl.ANY), pl.BlockSpec(memory_space=pl.ANY),
                  seq(rows, SEG), seq(rows, SEG), seq(rows, SEG), seq(n_pages + 1, rows, LANES), seq(rows, LANES)],
        out_specs=seq(rows, SEG),
        scratch_shapes=[buf, buf, pltpu.SemaphoreType.DMA((2,)), pltpu.SemaphoreType.DMA((2,))],
    )
    vmem = 2 * 2 * gp * page_rows * D_HEAD * 4 + (16 << 20)
    return pl.pallas_call(
        functools.partial(_sample_attn_kernel, n_pages=n_pages),
        grid_spec=grid_spec,
        out_shape=jax.ShapeDtypeStruct((b, rows, SEG), F32),
        compiler_params=_vmem_params(vmem, ("arbitrary",)),
        name="sample_attn",
    )(page_table, cache_k, cache_v, aq, k_new, v_new, scores, thr)


def _merge_kernel(x_ref, oh_ref, gh_ref, oa_ref, ga_ref, gnh_ref, gna_ref, wo_ref, lng_ref, lnb_ref, y_ref):
    def branch(o_ref, gate_ref, gain_ref):
        parts = []
        for h in range(H_ATT):
            o = o_ref[:, h * D_HEAD:(h + 1) * D_HEAD]
            parts.append(o * lax.rsqrt(jnp.mean(o * o, axis=-1, keepdims=True) + EPS))
        g = gate_ref[...]
        return (jnp.concatenate(parts, axis=1) * gain_ref[...]) * (g * _sigmoid(g))

    hh = branch(oh_ref, gh_ref, gnh_ref).astype(BF16)
    aa = branch(oa_ref, ga_ref, gna_ref).astype(BF16)
    mix = (jnp.dot(hh, wo_ref[:SEG, :], preferred_element_type=F32)
           + jnp.dot(aa, wo_ref[SEG:, :], preferred_element_type=F32))
    r = ALPHA * x_ref[...] + mix
    mu = jnp.mean(r, axis=-1, keepdims=True)
    rc = r - mu
    var = jnp.mean(rc * rc, axis=-1, keepdims=True)
    y_ref[...] = rc * lax.rsqrt(var + EPS) * lng_ref[...] + lnb_ref[...]


def _merge(x2, oh, gh, oa, ga, gnh, gna, wo, lng, lnb, *, tm):
    t, d = x2.shape
    assert t % tm == 0
    row = lambda w: pl.BlockSpec((tm, w), lambda i: (i, 0))
    return pl.pallas_call(
        _merge_kernel,
        grid=(t // tm,),
        in_specs=[row(d), row(SEG), row(SEG), row(SEG), row(SEG), _resident(gnh.shape), _resident(gna.shape),
                  _resident(wo.shape), _resident(lng.shape), _resident(lnb.shape)],
        out_specs=row(d),
        out_shape=jax.ShapeDtypeStruct((t, d), F32),
        compiler_params=_vmem_params(40 << 20, ("parallel",)),
        name="merge",
    )(x2, oh, gh, oa, ga, gnh, gna, wo, lng, lnb)


def kernel(x_prompt, x_sample, cache_k, cache_v, cache_kidx, state_hgrn, page_table,
           w_in, w_out, lb_logits, g_hgrn, g_attn, ln_g, ln_b):
    bp, t, d = x_prompt.shape
    bs, ts, _ = x_sample.shape
    n_pages = page_table.shape[1]
    page = cache_k.shape[2]
    assert w_in.shape[0] == DEPTH and bp == 1 and ts <= SAMPLE_ROWS
    main = N_SEG * SEG

    w = w_in[0]
    wm = w[:, :main].astype(BF16)
    wt = jnp.pad(w[:, main:], ((0, 0), (0, LANES - (D_IDX + H_IDX)))).astype(BF16)
    seg_cols = lambda j: w[:, j * SEG:(j + 1) * SEG]
    wx = jnp.concatenate([seg_cols(4), seg_cols(8), seg_cols(6), w[:, main + D_IDX:]], axis=1).T.astype(BF16)
    wo = w_out[0].astype(BF16)
    gnh, gna = g_hgrn[0][None, :], g_attn[0][None, :]
    lng, lnb = ln_g[0][None, :], ln_b[0][None, :]

    xp = x_prompt.reshape(t, d)
    kc = 2 * LANES
    (q, kk, v, lf, gh, ak, av, ag, ik, kbf, kib, aqt, qit, iwt, vt) = _project(
        xp, wm, wt, wx, lb_logits, tm=256, kc=kc)
    rs = lambda a: a.reshape(1, t, SEG)
    s0 = jnp.zeros((1, H_HGRN, D_HEAD, D_HEAD), F32)
    o_h, s_p = _hgrn(rs(q), rs(kk), rs(v), rs(lf), s0, rows=256, t_valid=HGRN_CHUNK)
    o_a = _dsa_prompt(aqt, qit, iwt, kbf, kib, vt, nsel=min(TOPK_MAX, t // 4))
    y_p = _merge(xp, o_h.reshape(t, SEG), gh, o_a, ag, gnh, gna, wo, lng, lnb, tm=512)

    rows = SAMPLE_ROWS
    xs = jnp.pad(x_sample, ((0, 0), (0, rows - ts), (0, 0))).reshape(bs * rows, d)
    (q, kk, v, lf, gh, ak_s, av_s, ag, ik_s, aq, iq, iw) = _project(
        xs, wm, wt, wx, lb_logits, tm=bs * rows, kc=0)
    rs = lambda a: a.reshape(bs, rows, a.shape[-1])
    o_h, s_s = _hgrn(rs(q), rs(kk), rs(v), rs(lf), state_hgrn[0], rows=rows, t_valid=ts)
    nsel = min(TOPK_MAX, (n_pages * page + ts) // 4)
    kidx_t = jnp.swapaxes(cache_kidx[0], 1, 2)
    scores, thr = _sample_scores(page_table, kidx_t, rs(iq), rs(iw), rs(ik_s), nsel=nsel, t_valid=ts)
    pages = lambda c: c[0].reshape(c.shape[1], page * H_ATT, D_HEAD)
    o_a = _sample_attn(page_table, pages(cache_k), pages(cache_v), rs(aq), rs(ak_s), rs(av_s), scores, thr)
    y_s = _merge(xs, o_h.reshape(bs * rows, SEG), gh, o_a.reshape(bs * rows, SEG), ag,
                 gnh, gna, wo, lng, lnb, tm=bs * rows)

    heads = lambda a, n: a.reshape(1, 1, n, H_ATT, D_HEAD)
    sample = lambda a: a.reshape(bs, rows, -1)[:, :ts]
    return (y_p.reshape(1, t, d),
            sample(y_s),
            heads(ak, t), heads(av, t), ik.reshape(1, 1, t, D_IDX), s_p[None],
            sample(ak_s).reshape(1, bs, ts, H_ATT, D_HEAD), sample(av_s).reshape(1, bs, ts, H_ATT, D_HEAD),
            sample(ik_s)[None], s_s[None])
```

```python
import functools

import jax
import jax.numpy as jnp
from jax import lax
from jax.experimental import pallas as pl
from jax.experimental.pallas import tpu as pltpu

F32 = jnp.float32
BF16 = jnp.bfloat16
I32 = jnp.int32

H_HGRN = 4
H_ATT = 4
D_HEAD = 128
H_IDX = 8
D_IDX = 64
SEG = 512
N_SEG = 9
TOPK_MAX = 256
HGRN_CHUNK = 64
Q_BLOCK = 256
PROMPT_KEY_CHUNK = 512
EPS = 1e-5
DEPTH = 1
ALPHA = (2 * DEPTH) ** 0.25
IDX_SCALE = H_IDX ** -0.5 * D_IDX ** -0.5
ATT_SCALE = D_HEAD ** -0.5
LOG2E = 1.4426950408889634

LANES = 128
SUBLANES = 8
SAMPLE_ROWS = SUBLANES

INT_MAX = 2 ** 31 - 1
KEY_NEG_INF = -2139095041
KEY_POS_INF = 2139095040
ZERO_KEY_MIN = -(2 ** 23)
ZERO_KEY_MAX = 2 ** 23 - 1
NEG_BIG = -1e30
MAX_BISECT_STEPS = 34
TIE_GROUP_KEYS = 1

SCORE_PAGES_PER_GROUP = 32
SCORE_PAGES_PER_DOT = 8
ATTN_PAGES_PER_GROUP = 8
PAGE_SLOTS = 4

NT_DIMS = (((1,), (1,)), ((), ()))
TN_DIMS = (((0,), (0,)), ((), ()))


def _sigmoid(x):
    return 1.0 / (1.0 + jnp.exp(-x))


def _vmem_params(nbytes, semantics):
    return pltpu.CompilerParams(dimension_semantics=semantics, vmem_limit_bytes=int(nbytes))


def _resident(shape):
    zeros = (0,) * len(shape)
    return pl.BlockSpec(shape, lambda *_: zeros, pipeline_mode=pl.Buffered(1))


def _lane_sum(x):
    return jnp.broadcast_to(jnp.sum(x, axis=1, keepdims=True), x.shape)


def _sublane_all(op, x):
    return jnp.broadcast_to(op(x, axis=0, keepdims=True), x.shape)


def _pad_rows(a, n):
    return jnp.concatenate([a, jnp.zeros((n - a.shape[0], a.shape[1]), a.dtype)], axis=0)


def _proj_kernel(x_ref, wm_ref, wt_ref, wx_ref, lbl_ref, *out_refs, kc):
    x = x_ref[...].astype(BF16)

    def seg(j):
        return jnp.dot(x, wm_ref[:, j * SEG:(j + 1) * SEG], preferred_element_type=F32)

    q_ref, kk_ref, v_ref, lf_ref, gh_ref, ak_ref, av_ref, ag_ref, ik_ref = out_refs[:9]
    hq = seg(0)
    q_ref[...] = hq * _sigmoid(hq)
    lbl = lbl_ref[...]
    e = jnp.exp(lbl - jnp.max(lbl, axis=0, keepdims=True))
    lb = e[0:1] / jnp.sum(e, axis=0, keepdims=True)
    f = lb + (1.0 - lb) * _sigmoid(seg(1))
    kk_ref[...] = 1.0 - f
    lf_ref[...] = jnp.log(f)
    v_ref[...] = seg(2)
    gh_ref[...] = seg(3)
    ak = seg(5)
    av = seg(6)
    if kc == 0:
        ak_ref[...] = ak
        av_ref[...] = av
    else:
        for h in range(H_ATT):
            ak_ref[pl.ds(h, ak.shape[0], stride=H_ATT), :] = ak[:, h * D_HEAD:(h + 1) * D_HEAD]
            av_ref[pl.ds(h, av.shape[0], stride=H_ATT), :] = av[:, h * D_HEAD:(h + 1) * D_HEAD]
    ag_ref[...] = seg(7)
    tail = jnp.dot(x, wt_ref[...], preferred_element_type=F32)
    ik = tail[:, :D_IDX]
    ik_ref[...] = ik
    if kc == 0:
        aq_ref, iq_ref, iw_ref = out_refs[9:]
        aq_ref[...] = seg(4)
        iq_ref[...] = seg(8)
        iw_ref[...] = tail[:, D_IDX:D_IDX + H_IDX]
    else:
        kbf_ref, kib_ref, aqt_ref, qit_ref, iwt_ref, vt_ref = out_refs[9:]
        kbf_ref[...] = ak.astype(BF16)
        kib_ref[...] = ik.astype(BF16)
        xt = lax.dot_general(wx_ref[...], x, NT_DIMS, preferred_element_type=F32)
        aqt_ref[...] = (xt[:SEG] * (ATT_SCALE * LOG2E)).astype(BF16)
        qit_ref[...] = xt[SEG:2 * SEG].astype(BF16)
        iwt_ref[...] = xt[3 * SEG:] * IDX_SCALE
        vt = xt[2 * SEG:3 * SEG].astype(BF16)
        for j in range(vt.shape[1] // kc):
            vt_ref[j] = vt[:, j * kc:(j + 1) * kc]


def _project(x2, wm, wt, wx, lbl, *, tm, kc):
    t, d = x2.shape
    assert t % tm == 0 and (kc == 0 or tm % kc == 0)
    row = lambda w: pl.BlockSpec((tm, w), lambda i: (i, 0))
    col = lambda h: pl.BlockSpec((h, tm), lambda i: (0, i))
    f32 = lambda *s: jax.ShapeDtypeStruct(s, F32)
    b16 = lambda *s: jax.ShapeDtypeStruct(s, BF16)
    out_shape = [f32(t, SEG)] * 8 + [f32(t, D_IDX)]
    out_specs = [row(SEG)] * 8 + [row(D_IDX)]
    if kc:
        out_shape[5:7] = [f32(t * H_ATT, D_HEAD)] * 2
        out_specs[5:7] = [pl.BlockSpec((tm * H_ATT, D_HEAD), lambda i: (i, 0))] * 2
    if kc == 0:
        out_shape += [f32(t, SEG), f32(t, SEG), f32(t, H_IDX)]
        out_specs += [row(SEG), row(SEG), row(H_IDX)]
    else:
        out_shape += [b16(t, SEG), b16(t, D_IDX), b16(SEG, t), b16(SEG, t), f32(H_IDX, t), b16(t // kc, SEG, kc)]
        out_specs += [row(SEG), row(D_IDX), col(SEG), col(SEG), col(H_IDX),
                      pl.BlockSpec((tm // kc, SEG, kc), lambda i: (i, 0, 0))]
    vmem = (wm.size + wx.size) * 2 + 2 * tm * d * 4 + 2 * tm * SEG * 12 * 4 + (8 << 20)
    return pl.pallas_call(
        functools.partial(_proj_kernel, kc=kc),
        grid=(t // tm,),
        in_specs=[row(d), _resident(wm.shape), _resident(wt.shape), _resident(wx.shape), _resident(lbl.shape)],
        out_specs=out_specs,
        out_shape=out_shape,
        compiler_params=_vmem_params(vmem, ("parallel",)),
        name="proj",
    )(x2, wm, wt, wx, lbl)


def _hgrn_kernel(q_ref, kk_ref, v_ref, lf_ref, s0_ref, o_ref, sout_ref, st_ref, *, rows, t_valid):
    c_len = HGRN_CHUNK
    step = pl.program_id(1)

    @pl.when(step == 0)
    def _():
        for h in range(H_HGRN):
            st_ref[h] = s0_ref[0, h].T

    r = lax.broadcasted_iota(I32, (c_len, c_len), 0)
    c = lax.broadcasted_iota(I32, (c_len, c_len), 1)
    causal = r >= c
    tri = causal.astype(F32)
    pad = c_len - rows if rows < c_len else 0

    def load(ref, r0, n, h):
        a = ref[0, r0:r0 + n, h * D_HEAD:(h + 1) * D_HEAD]
        if pad:
            a = jnp.concatenate([a, jnp.zeros((pad, D_HEAD), F32)], axis=0)
        return a

    n_rows = min(rows, c_len)
    n_chunks = max(rows // c_len, 1)
    tiles = [(ci, h) for ci in range(n_chunks) for h in range(H_HGRN)]

    def log_decay(ci, h):
        lf = load(lf_ref, ci * c_len, n_rows, h)
        if t_valid < c_len:
            lf = jnp.where(lax.broadcasted_iota(I32, lf.shape, 0) < t_valid, lf, 0.0)
        return jnp.dot(tri, lf, precision=lax.Precision.HIGHEST, preferred_element_type=F32)

    b = {k: log_decay(*k) for k in tiles}
    b_end = {k: b[k][c_len - 1:c_len, :] for k in tiles}
    qd = {(ci, h): (load(q_ref, ci * c_len, n_rows, h) * jnp.exp(b[ci, h])).astype(BF16) for ci, h in tiles}
    kk = {(ci, h): load(kk_ref, ci * c_len, n_rows, h) for ci, h in tiles}
    kd = {k: (kk[k] * jnp.exp(-b[k])).astype(BF16) for k in tiles}
    k2 = {k: (kk[k] * jnp.exp(b_end[k] - b[k])).astype(BF16) for k in tiles}
    vb = {(ci, h): load(v_ref, ci * c_len, n_rows, h).astype(BF16) for ci, h in tiles}
    att = {k: lax.dot_general(qd[k], kd[k], NT_DIMS, preferred_element_type=F32) for k in tiles}
    att = {k: jnp.where(causal, att[k], 0.0).astype(BF16) for k in tiles}
    kv = {k: lax.dot_general(vb[k], k2[k], TN_DIMS, preferred_element_type=F32) for k in tiles}
    o_intra = {k: jnp.dot(att[k], vb[k], preferred_element_type=F32) for k in tiles}
    pre = {k: (qd[k], o_intra[k], kv[k], jnp.exp(b_end[k])) for k in tiles}
    for h in range(H_HGRN):
        st = st_ref[h]
        for ci in range(n_chunks):
            qd, o_intra, kv, decay = pre[ci, h]
            o = o_intra + lax.dot_general(qd, st.astype(BF16), NT_DIMS, preferred_element_type=F32)
            o_ref[0, ci * c_len:ci * c_len + n_rows, h * D_HEAD:(h + 1) * D_HEAD] = o[:n_rows]
            st = st * decay + kv
        st_ref[h] = st

    @pl.when(step == pl.num_programs(1) - 1)
    def _():
        for h in range(H_HGRN):
            sout_ref[0, h] = st_ref[h].T


def _hgrn(q, kk, v, lf, s0, *, rows, t_valid):
    b, t, _ = q.shape
    assert t % rows == 0
    blk = pl.BlockSpec((1, rows, SEG), lambda i, j: (i, j, 0))
    sblk = pl.BlockSpec((1, H_HGRN, D_HEAD, D_HEAD), lambda i, j: (i, 0, 0, 0))
    return pl.pallas_call(
        functools.partial(_hgrn_kernel, rows=rows, t_valid=t_valid),
        grid=(b, t // rows),
        in_specs=[blk, blk, blk, blk, sblk],
        out_specs=[blk, sblk],
        out_shape=[jax.ShapeDtypeStruct((b, t, SEG), F32),
                   jax.ShapeDtypeStruct((b, H_HGRN, D_HEAD, D_HEAD), F32)],
        scratch_shapes=[pltpu.VMEM((H_HGRN, D_HEAD, D_HEAD), F32)],
        compiler_params=_vmem_params(32 << 20, ("parallel", "arbitrary")),
        name="hgrn",
    )(q, kk, v, lf, s0)


def _is_zero_key(k):
    return jnp.logical_and(k >= ZERO_KEY_MIN, k <= ZERO_KEY_MAX)


def _key_to_float(k):
    f = lax.bitcast_convert_type(k ^ (lax.shift_right_arithmetic(k, 31) & INT_MAX), F32)
    return jnp.where(_is_zero_key(k), 0.0, f)


def _next_key(k, step=1):
    return jnp.where(_is_zero_key(k), jnp.maximum(k + step, ZERO_KEY_MAX + 1), k + step)


def _tie_group(lo):
    g = TIE_GROUP_KEYS
    start = lo if g == 1 else jnp.where(lo > KEY_NEG_INF + g, lo & (-g), lo)
    return start, _next_key(jnp.minimum(start, KEY_POS_INF + 1 - g), g)


def _bisect_threshold(count_ge, nsel, lo0, hi0):
    def active(lo, hi):
        return (lo + 1) != hi

    def cond(state):
        it, lo, hi, _ = state
        return jnp.logical_and(it < MAX_BISECT_STEPS,
                               jnp.max(jnp.where(active(lo, hi), 1.0, 0.0)) > 0.0)

    def body(state):
        it, lo, hi, clo = state
        mid = (lo & hi) + lax.shift_right_arithmetic(lo ^ hi, 1)
        cnt = count_ge(_key_to_float(mid))
        ge = cnt >= nsel
        exact = cnt == nsel
        act = active(lo, hi)
        new_hi = jnp.where(exact, mid + 1, jnp.where(ge, hi, mid))
        return (it + 1,
                jnp.where(jnp.logical_and(act, ge), mid, lo),
                jnp.where(act, new_hi, hi),
                jnp.where(jnp.logical_and(act, ge), cnt, clo))

    _, lo, _, clo = lax.while_loop(cond, body, (jnp.int32(0), lo0, hi0, jnp.zeros(lo0.shape, F32)))
    return lo, clo


def _topk_threshold_rows(sc_ref, nch, nsel, rows, rows_valid):
    shape = (rows, LANES)

    def count_ge(thr):
        parts = [jnp.zeros(shape, F32)] * 4
        for ci in range(nch):
            parts[ci % 4] = parts[ci % 4] + jnp.where(sc_ref[ci] >= thr, 1.0, 0.0)
        return _lane_sum((parts[0] + parts[1]) + (parts[2] + parts[3]))

    real = lax.broadcasted_iota(I32, shape, 0) < rows_valid
    lo, clo = _bisect_threshold(count_ge, nsel, jnp.full(shape, KEY_NEG_INF, I32),
                                jnp.where(real, KEY_POS_INF + 1, KEY_NEG_INF + 1))
    lo, group_end = _tie_group(lo)
    if TIE_GROUP_KEYS > 1:
        clo = jnp.where(real, count_ge(_key_to_float(lo)), 0.0)
    t = _key_to_float(lo)

    @pl.when(jnp.max(clo) > nsel)
    def _():
        t_next = _key_to_float(group_end)
        need = nsel - count_ge(t_next)
        zero = jnp.zeros(shape, F32)

        def residual(s):
            return jnp.where(jnp.logical_and(s >= t, s < t_next), s - t, jnp.nan)

        def count_res(pred):
            return _lane_sum(functools.reduce(
                jnp.add, [jnp.where(pred(residual(sc_ref[ci])), 1.0, 0.0) for ci in range(nch)]))

        key0 = jnp.zeros(shape, I32)
        lo2, _ = lax.cond(jnp.max(count_res(lambda r: r > 0.0)) > 0.0,
                          lambda: _bisect_threshold(lambda thr: count_res(lambda r: r >= thr), need, key0,
                                                    jnp.full(shape, KEY_POS_INF + 1, I32)),
                          lambda: (key0, zero))
        t2 = _key_to_float(lo2)
        t2_next = _key_to_float(_next_key(lo2))
        need2 = need - count_res(lambda r: r >= t2_next)
        strict_upper = (lax.broadcasted_iota(I32, (LANES, LANES), 0)
                        < lax.broadcasted_iota(I32, (LANES, LANES), 1)).astype(BF16)

        def tie_body(ci, seen):
            s = sc_ref[ci]
            res = residual(s)
            tied = jnp.logical_and(res >= t2, res < t2_next)
            tiedf = jnp.where(tied, 1.0, 0.0)
            before = seen + jnp.dot(tiedf.astype(BF16), strict_upper, preferred_element_type=F32)
            lose = jnp.logical_or(res < t2, jnp.logical_and(tied, before >= need2))
            sc_ref[ci] = jnp.where(lose, jnp.nan, s)
            return seen + _lane_sum(tiedf)

        lax.fori_loop(0, nch, tie_body, zero)

    return t


def _topk_threshold_cols(sc_ref, nch, nsel, kc):
    ql = sc_ref.shape[2]
    shape = (SUBLANES, ql)
    nv = kc // SUBLANES
    zero = jnp.zeros(shape, F32)

    def count_ge(thr):
        def body(ci, parts):
            parts = list(parts)
            for v in range(nv):
                s = sc_ref[ci, v * SUBLANES:(v + 1) * SUBLANES, :]
                parts[v % 4] = parts[v % 4] + jnp.where(s >= thr, 1.0, 0.0)
            return tuple(parts)
        parts = lax.fori_loop(0, nch, body, (zero,) * 4)
        return _sublane_all(jnp.sum, (parts[0] + parts[1]) + (parts[2] + parts[3]))

    def reduce_tiles(fn, op, init):
        def body(ci, parts):
            parts = list(parts)
            for v in range(nv):
                s = sc_ref[ci, v * SUBLANES:(v + 1) * SUBLANES, :]
                parts[v % 4] = op(parts[v % 4], fn(s))
            return tuple(parts)
        parts = lax.fori_loop(0, nch, body, (init,) * 4)
        return op(op(parts[0], parts[1]), op(parts[2], parts[3]))

    lo, clo = _bisect_threshold(count_ge, nsel, jnp.full(shape, KEY_NEG_INF, I32),
                                jnp.full(shape, KEY_POS_INF + 1, I32))
    lo, group_end = _tie_group(lo)
    if TIE_GROUP_KEYS > 1:
        clo = count_ge(_key_to_float(lo))
    t = _key_to_float(lo)

    @pl.when(jnp.max(clo) > nsel)
    def _():
        t_next = _key_to_float(group_end)
        need = nsel - count_ge(t_next)

        def residual(s, t_, t_next_):
            return jnp.where(jnp.logical_and(s >= t_, s < t_next_), s - t_, jnp.nan)

        def count_res_ge(thr):
            return _sublane_all(jnp.sum, reduce_tiles(
                lambda s: jnp.where(residual(s, t, t_next) >= thr, 1.0, 0.0), jnp.add, zero))

        rmax = _sublane_all(jnp.max, reduce_tiles(
            lambda s: jnp.where(residual(s, t, t_next) > 0.0, 1.0, 0.0), jnp.maximum, zero))
        key0 = jnp.zeros(shape, I32)
        lo2, _ = lax.cond(jnp.max(rmax) > 0.0,
                          lambda: _bisect_threshold(count_res_ge, need, key0, jnp.full(shape, KEY_POS_INF + 1, I32)),
                          lambda: (key0, zero))
        t2 = _key_to_float(lo2)
        t2_next = _key_to_float(_next_key(lo2))
        need2 = (need - count_res_ge(t2_next))[0:1]
        t_r, tn_r, t2_r, t2n_r = t[0:1], t_next[0:1], t2[0:1], t2_next[0:1]
        strict_lower = (lax.broadcasted_iota(I32, (kc, kc), 1)
                        < lax.broadcasted_iota(I32, (kc, kc), 0)).astype(BF16)

        def tie_body(ci, seen):
            s = sc_ref[ci]
            res = residual(s, t_r, tn_r)
            tied = jnp.logical_and(res >= t2_r, res < t2n_r)
            tiedf = jnp.where(tied, 1.0, 0.0)
            before = seen + jnp.dot(strict_lower, tiedf.astype(BF16), preferred_element_type=F32)
            lose = jnp.logical_or(res < t2_r, jnp.logical_and(tied, before >= need2))
            sc_ref[ci] = jnp.where(lose, jnp.nan, s)
            return seen + jnp.sum(tiedf, axis=0, keepdims=True)

        lax.fori_loop(0, nch, tie_body, jnp.zeros((1, ql), F32))

    return t


def _dsa_prompt_kernel(aqt_ref, qit_ref, iwt_ref, kbf_ref, kib_ref, vt_ref, o_ref, sc_ref, acc_ref, *, nsel, kc):
    qb = ql = o_ref.shape[0]
    vc = vt_ref.shape[2]
    nv = kc // SUBLANES
    i = pl.program_id(0)
    nch = 2 * (((i + 1) * qb + 2 * kc - 1) // (2 * kc))
    n_full = (i * qb + 1) // kc

    qit = qit_ref[...]
    rhs = jnp.concatenate([qit[h * D_IDX:(h + 1) * D_IDX, :] for h in range(H_IDX)], axis=1)
    w = iwt_ref[...]

    def scores(ci):
        parts = []
        for j in range(kc // vc):
            r0 = pl.multiple_of(ci * kc + j * vc, vc)
            s = jnp.dot(kib_ref[pl.ds(r0, vc), :], rhs, preferred_element_type=F32)
            acc = jnp.zeros((vc, ql), F32)
            for h in range(H_IDX):
                acc = acc + jnp.maximum(s[:, h * ql:(h + 1) * ql], 0.0) * w[h:h + 1, :]
            parts.append(acc)
        return jnp.concatenate(parts, axis=0)

    def score_body(ci, carry):
        sc_ref[ci] = scores(ci)
        return carry

    def masked_score_body(ci, carry):
        kpos = ci * kc + lax.broadcasted_iota(I32, (kc, ql), 0)
        qpos = i * qb + lax.broadcasted_iota(I32, (kc, ql), 1)
        sc_ref[ci] = jnp.where(kpos <= qpos, scores(ci), jnp.nan)
        return carry

    lax.fori_loop(0, n_full, score_body, 0)
    lax.fori_loop(n_full, nch, masked_score_body, 0)

    t = _topk_threshold_cols(sc_ref, nch, nsel, kc)

    aqt = aqt_ref[...]
    qh = [aqt[h * D_HEAD:(h + 1) * D_HEAD, :] for h in range(H_ATT)]
    zq = jnp.zeros((D_HEAD, ql), BF16)
    qpair = [jnp.concatenate([jnp.concatenate([qh[2 * p], zq], axis=1),
                              jnp.concatenate([zq, qh[2 * p + 1]], axis=1)], axis=0) for p in range(H_ATT // 2)]
    for h in range(H_ATT):
        acc_ref[h] = jnp.zeros((D_HEAD, ql), F32)

    def tile(x):
        return jnp.broadcast_to(x[None], (nv, SUBLANES, ql)).reshape(kc, ql)

    heads = range(H_ATT)

    def logits(ci):
        r0 = pl.multiple_of(ci * kc, kc)
        return [jnp.dot(kbf_ref[pl.ds(r0, kc), p * 2 * D_HEAD:(p + 1) * 2 * D_HEAD], qpair[p],
                        preferred_element_type=F32) for p in range(H_ATT // 2)]

    def softmax_step(ci, lg_pair, carry):
        m, l = carry
        bias = jnp.where(sc_ref[ci] >= tile(t), 0.0, NEG_BIG)
        lg = [lg_pair[h // 2][:, (h % 2) * ql:(h % 2 + 1) * ql] + bias for h in heads]
        mx = [jnp.max(lg[h].reshape(nv, SUBLANES, ql), axis=0) for h in heads]
        m_new = [jnp.maximum(m[h], _sublane_all(jnp.max, mx[h])) for h in heads]
        alpha = [jnp.exp2(m[h] - m_new[h]) for h in heads]
        p = [jnp.exp2(lg[h] - tile(m_new[h])) for h in heads]
        l_new = [alpha[h] * l[h] + jnp.sum(p[h].reshape(nv, SUBLANES, ql), axis=0) for h in heads]
        return (m_new, l_new), alpha, [p[h].astype(BF16) for h in heads]

    def accumulate(ci, alpha, pb):
        pv = [functools.reduce(jnp.add, [
            jnp.dot(vt_ref[ci * (kc // vc) + j, h * D_HEAD:(h + 1) * D_HEAD, :], pb[h][j * vc:(j + 1) * vc],
                    preferred_element_type=F32) for j in range(kc // vc)]) for h in heads]
        for h in heads:
            acc_ref[h] = acc_ref[h] * jnp.broadcast_to(alpha[h][0:1], (D_HEAD, ql)) + pv[h]

    def att_body(pi, carry):
        ca, cb = 2 * pi, 2 * pi + 1
        lg_a = logits(ca)
        lg_b = logits(cb)
        carry, alpha_a, pb_a = softmax_step(ca, lg_a, carry)
        accumulate(ca, alpha_a, pb_a)
        carry, alpha_b, pb_b = softmax_step(cb, lg_b, carry)
        accumulate(cb, alpha_b, pb_b)
        return carry

    init = ([jnp.full((SUBLANES, ql), NEG_BIG, F32)] * H_ATT, [jnp.zeros((SUBLANES, ql), F32)] * H_ATT)
    _, l = lax.fori_loop(0, nch // 2, att_body, init)
    for h in range(H_ATT):
        denom = jnp.broadcast_to(jnp.sum(l[h], axis=0, keepdims=True), (D_HEAD, ql))
        o_ref[:, h * D_HEAD:(h + 1) * D_HEAD] = (acc_ref[h] / denom).T


def _dsa_prompt(aqt, qit, iwt, kbf, kib, vt, *, nsel):
    t = kbf.shape[0]
    vc = vt.shape[2]
    kc = PROMPT_KEY_CHUNK
    nchunks = t // kc
    qb = Q_BLOCK
    assert t % qb == 0 and t % (2 * kc) == 0 and kc % vc == 0 and vt.shape[0] * vc == t
    col = lambda h: pl.BlockSpec((h, qb), lambda i: (0, i))
    vmem = kbf.size * 2 * 2 + t * LANES * 2 + nchunks * kc * qb * 4 + (6 << 20)
    return pl.pallas_call(
        functools.partial(_dsa_prompt_kernel, nsel=nsel, kc=kc),
        grid=(t // qb,),
        in_specs=[col(SEG), col(SEG), col(H_IDX), _resident(kbf.shape), _resident(kib.shape), _resident(vt.shape)],
        out_specs=pl.BlockSpec((qb, SEG), lambda i: (i, 0)),
        out_shape=jax.ShapeDtypeStruct((t, SEG), F32),
        scratch_shapes=[pltpu.VMEM((nchunks, kc, qb), F32), pltpu.VMEM((H_ATT, D_HEAD, qb), F32)],
        compiler_params=_vmem_params(vmem, ("arbitrary",)),
        name="dsa_prompt",
    )(aqt, qit, iwt, kbf, kib, vt)


def _page_group_copies(hbm_refs, bufs, sems, pt_ref, seq, group, slot, pages_per_group):
    copies = []
    for hbm, buf, sem in zip(hbm_refs, bufs, sems):
        for j in range(pages_per_group):
            page = pt_ref[seq, group * pages_per_group + j]
            copies.append(pltpu.make_async_copy(hbm.at[page], buf.at[slot, j], sem.at[slot]))
    return copies


def _stream_page_groups(hbm_refs, bufs, sems, pt_ref, n_groups, pages_per_group, consume, carry):
    n_slots = PAGE_SLOTS
    ahead = n_slots - 1
    assert n_groups % n_slots == 0 and ahead <= n_groups
    b = pl.program_id(0)
    n_seq = pl.num_programs(0)
    copies = functools.partial(_page_group_copies, hbm_refs, bufs, sems, pt_ref,
                               pages_per_group=pages_per_group)

    @pl.when(b == 0)
    def _():
        for g0 in range(ahead):
            for c in copies(0, g0, g0):
                c.start()

    def ring_body(i, carry):
        for slot in range(n_slots):
            g = n_slots * i + slot
            nxt = g + ahead
            nxt_slot = (slot + ahead) % n_slots

            @pl.when(nxt < n_groups)
            def _():
                for c in copies(b, nxt, nxt_slot):
                    c.start()

            @pl.when(jnp.logical_and(nxt >= n_groups, b + 1 < n_seq))
            def _():
                for c in copies(b + 1, nxt - n_groups, nxt_slot):
                    c.start()

            for c in copies(b, g, slot):
                c.wait()
            carry = consume(g, slot, carry)
        return carry

    return lax.fori_loop(0, n_groups // n_slots, ring_body, carry)


def _sample_scores_kernel(pt_ref, kidx_hbm, qi_ref, wi_ref, kin_ref, sc_ref, thr_ref, kbuf, sem,
                          *, n_pages, nsel, t_valid):
    rows = SAMPLE_ROWS
    gp, dp = SCORE_PAGES_PER_GROUP, SCORE_PAGES_PER_DOT
    qi = qi_ref[0]
    qhm = jnp.concatenate([qi[:, h * D_IDX:(h + 1) * D_IDX] for h in range(H_IDX)], axis=0).astype(BF16)
    wi = wi_ref[0] * IDX_SCALE

    def head_sum(s):
        acc = jnp.zeros((rows, s.shape[1]), F32)
        for h in range(H_IDX):
            acc = acc + jnp.maximum(s[h * rows:(h + 1) * rows], 0.0) * wi[:, h:h + 1]
        return acc

    def consume(g, slot, carry):
        for d in range(gp // dp):
            kt = jnp.concatenate([kbuf[slot, d * dp + j] for j in range(dp)], axis=1).astype(BF16)
            acc = head_sum(jnp.dot(qhm, kt, preferred_element_type=F32))
            for j in range(dp):
                sc_ref[0, g * gp + d * dp + j] = acc[:, j * LANES:(j + 1) * LANES]
        return carry

    _stream_page_groups([kidx_hbm], [kbuf], [sem], pt_ref, n_pages // gp, gp, consume, 0)

    kn = _pad_rows(kin_ref[0], LANES).astype(BF16)
    r = lax.broadcasted_iota(I32, (rows, LANES), 0)
    c = lax.broadcasted_iota(I32, (rows, LANES), 1)
    acc = head_sum(lax.dot_general(qhm, kn, NT_DIMS, preferred_element_type=F32))
    sc_ref[0, n_pages] = jnp.where(jnp.logical_and(c <= r, c < t_valid), acc, jnp.nan)
    thr_ref[0] = _topk_threshold_rows(sc_ref.at[0], n_pages + 1, nsel, rows, t_valid)


def _sample_scores(page_table, kidx_t, qi, wi, ki_new, *, nsel, t_valid):
    b, n_pages = page_table.shape
    _, _, page = kidx_t.shape
    gp = SCORE_PAGES_PER_GROUP
    assert page == LANES and n_pages % (PAGE_SLOTS * gp) == 0
    rows = SAMPLE_ROWS
    seq = lambda *tail: pl.BlockSpec((1,) + tail, lambda i, pt: (i,) + (0,) * len(tail))
    grid_spec = pltpu.PrefetchScalarGridSpec(
        num_scalar_prefetch=1,
        grid=(b,),
        in_specs=[pl.BlockSpec(memory_space=pl.ANY), seq(rows, SEG), seq(rows, H_IDX), seq(rows, D_IDX)],
        out_specs=[seq(n_pages + 1, rows, page), seq(rows, LANES)],
        scratch_shapes=[pltpu.VMEM((PAGE_SLOTS, gp, D_IDX, page), F32), pltpu.SemaphoreType.DMA((PAGE_SLOTS,))],
    )
    return pl.pallas_call(
        functools.partial(_sample_scores_kernel, n_pages=n_pages, nsel=nsel, t_valid=t_valid),
        grid_spec=grid_spec,
        out_shape=[jax.ShapeDtypeStruct((b, n_pages + 1, rows, page), F32),
                   jax.ShapeDtypeStruct((b, rows, LANES), F32)],
        compiler_params=_vmem_params(32 << 20, ("arbitrary",)),
        name="sample_scores",
    )(page_table, kidx_t, qi, wi, ki_new)


def _sample_attn_kernel(pt_ref, k_hbm, v_hbm, aq_ref, kn_ref, vn_ref, sc_ref, thr_ref, o_ref,
                        kbuf, vbuf, ksem, vsem, *, n_pages):
    rows = SAMPLE_ROWS
    gp = ATTN_PAGES_PER_GROUP
    page = LANES
    thr = thr_ref[0][:, 0:1]
    aq = aq_ref[0]
    qh = [aq[:, h * D_HEAD:(h + 1) * D_HEAD].astype(BF16) for h in range(H_ATT)]

    def head_rows(buf, slot, j, h):
        return buf[slot, j, pl.ds(h, page, stride=H_ATT), :].astype(BF16)

    def attend(carry, scores, keys, values):
        m, l, acc = carry
        sel = scores >= thr
        heads = range(H_ATT)
        lg = [lax.dot_general(qh[h], keys[h], NT_DIMS, preferred_element_type=F32) for h in heads]
        lg = [jnp.where(sel, lg[h] * ATT_SCALE, NEG_BIG) for h in heads]
        m_new = [jnp.maximum(m[h], jnp.max(lg[h], axis=1, keepdims=True)) for h in heads]
        alpha = [jnp.exp(m[h] - m_new[h]) for h in heads]
        p = [jnp.where(sel, jnp.exp(lg[h] - m_new[h]), 0.0) for h in heads]
        pv = [jnp.dot(p[h].astype(BF16), values[h], preferred_element_type=F32) for h in heads]
        return (m_new, [alpha[h] * l[h] + jnp.sum(p[h], axis=1, keepdims=True) for h in heads],
                [alpha[h] * acc[h] + pv[h] for h in heads])

    def consume(g, slot, carry):
        scores = jnp.concatenate([sc_ref[0, g * gp + j] for j in range(gp)], axis=1)
        keys = [jnp.concatenate([head_rows(kbuf, slot, j, h) for j in range(gp)], axis=0) for h in range(H_ATT)]
        values = [jnp.concatenate([head_rows(vbuf, slot, j, h) for j in range(gp)], axis=0) for h in range(H_ATT)]
        return attend(carry, scores, keys, values)

    init = ([jnp.full((rows, 1), NEG_BIG, F32)] * H_ATT, [jnp.zeros((rows, 1), F32)] * H_ATT,
            [jnp.zeros((rows, LANES), F32)] * H_ATT)
    carry = _stream_page_groups([k_hbm, v_hbm], [kbuf, vbuf], [ksem, vsem], pt_ref, n_pages // gp, gp,
                                consume, init)
    kn, vn = kn_ref[0], vn_ref[0]
    hs = lambda h: slice(h * D_HEAD, (h + 1) * D_HEAD)
    _, l, acc = attend(carry, sc_ref[0, n_pages],
                       [_pad_rows(kn[:, hs(h)], page).astype(BF16) for h in range(H_ATT)],
                       [_pad_rows(vn[:, hs(h)], page).astype(BF16) for h in range(H_ATT)])
    for h in range(H_ATT):
        o_ref[0, :, hs(h)] = acc[h] / l[h]


def _sample_attn(page_table, cache_k, cache_v, aq, k_new, v_new, scores, thr):
    b, n_pages = page_table.shape
    gp = ATTN_PAGES_PER_GROUP
    rows = SAMPLE_ROWS
    page_rows = cache_k.shape[1]
    assert page_rows == LANES * H_ATT and n_pages % (PAGE_SLOTS * gp) == 0
    seq = lambda *tail: pl.BlockSpec((1,) + tail, lambda i, pt: (i,) + (0,) * len(tail))
    buf = pltpu.VMEM((PAGE_SLOTS, gp, page_rows, D_HEAD), F32)
    dma_sems = pltpu.SemaphoreType.DMA((PAGE_SLOTS,))
    grid_spec = pltpu.PrefetchScalarGridSpec(
        num_scalar_prefetch=1,
        grid=(b,),
        in_specs=[pl.BlockSpec(memory_space=pl.ANY), pl.BlockSpec(memory_space=pl.ANY),
                  seq(rows, SEG), seq(rows, SEG), seq(rows, SEG), seq(n_pages + 1, rows, LANES), seq(rows, LANES)],
        out_specs=seq(rows, SEG),
        scratch_shapes=[buf, buf, dma_sems, dma_sems],
    )
    vmem = 2 * PAGE_SLOTS * gp * page_rows * D_HEAD * 4 + (16 << 20)
    return pl.pallas_call(
        functools.partial(_sample_attn_kernel, n_pages=n_pages),
        grid_spec=grid_spec,
        out_shape=jax.ShapeDtypeStruct((b, rows, SEG), F32),
        compiler_params=_vmem_params(vmem, ("arbitrary",)),
        name="sample_attn",
    )(page_table, cache_k, cache_v, aq, k_new, v_new, scores, thr)


def _merge_kernel(x_ref, oh_ref, gh_ref, oa_ref, ga_ref, gnh_ref, gna_ref, wo_ref, lng_ref, lnb_ref, y_ref):
    def branch(o_ref, gate_ref, gain_ref):
        parts = []
        for h in range(H_ATT):
            o = o_ref[:, h * D_HEAD:(h + 1) * D_HEAD]
            parts.append(o * lax.rsqrt(jnp.mean(o * o, axis=-1, keepdims=True) + EPS))
        g = gate_ref[...]
        return (jnp.concatenate(parts, axis=1) * gain_ref[...]) * (g * _sigmoid(g))

    hh = branch(oh_ref, gh_ref, gnh_ref).astype(BF16)
    aa = branch(oa_ref, ga_ref, gna_ref).astype(BF16)
    mix = (jnp.dot(hh, wo_ref[:SEG, :], preferred_element_type=F32)
           + jnp.dot(aa, wo_ref[SEG:, :], preferred_element_type=F32))
    r = ALPHA * x_ref[...] + mix
    mu = jnp.mean(r, axis=-1, keepdims=True)
    rc = r - mu
    var = jnp.mean(rc * rc, axis=-1, keepdims=True)
    y_ref[...] = rc * lax.rsqrt(var + EPS) * lng_ref[...] + lnb_ref[...]


def _merge(x2, oh, gh, oa, ga, gnh, gna, wo, lng, lnb, *, tm):
    t, d = x2.shape
    assert t % tm == 0
    row = lambda w: pl.BlockSpec((tm, w), lambda i: (i, 0))
    return pl.pallas_call(
        _merge_kernel,
        grid=(t // tm,),
        in_specs=[row(d), row(SEG), row(SEG), row(SEG), row(SEG), _resident(gnh.shape), _resident(gna.shape),
                  _resident(wo.shape), _resident(lng.shape), _resident(lnb.shape)],
        out_specs=row(d),
        out_shape=jax.ShapeDtypeStruct((t, d), F32),
        compiler_params=_vmem_params(40 << 20, ("parallel",)),
        name="merge",
    )(x2, oh, gh, oa, ga, gnh, gna, wo, lng, lnb)


def kernel(x_prompt, x_sample, cache_k, cache_v, cache_kidx, state_hgrn, page_table,
           w_in, w_out, lb_logits, g_hgrn, g_attn, ln_g, ln_b):
    bp, t, d = x_prompt.shape
    bs, ts, _ = x_sample.shape
    n_pages = page_table.shape[1]
    page = cache_k.shape[2]
    assert w_in.shape[0] == DEPTH and bp == 1 and ts <= SAMPLE_ROWS
    main = N_SEG * SEG

    w = w_in[0]
    wm = w[:, :main].astype(BF16)
    wt = jnp.pad(w[:, main:], ((0, 0), (0, LANES - (D_IDX + H_IDX)))).astype(BF16)
    seg_cols = lambda j: w[:, j * SEG:(j + 1) * SEG]
    wx = jnp.concatenate([seg_cols(4), seg_cols(8), seg_cols(6), w[:, main + D_IDX:]], axis=1).T.astype(BF16)
    wo = w_out[0].astype(BF16)
    gnh, gna = g_hgrn[0][None, :], g_attn[0][None, :]
    lng, lnb = ln_g[0][None, :], ln_b[0][None, :]

    xp = x_prompt.reshape(t, d)
    kc = 2 * LANES
    (q, kk, v, lf, gh, ak, av, ag, ik, kbf, kib, aqt, qit, iwt, vt) = _project(
        xp, wm, wt, wx, lb_logits, tm=256, kc=kc)
    rs = lambda a: a.reshape(1, t, SEG)
    s0 = jnp.zeros((1, H_HGRN, D_HEAD, D_HEAD), F32)
    o_h, s_p = _hgrn(rs(q), rs(kk), rs(v), rs(lf), s0, rows=256, t_valid=HGRN_CHUNK)
    o_a = _dsa_prompt(aqt, qit, iwt, kbf, kib, vt, nsel=min(TOPK_MAX, t // 4))
    y_p = _merge(xp, o_h.reshape(t, SEG), gh, o_a, ag, gnh, gna, wo, lng, lnb, tm=512)

    rows = SAMPLE_ROWS
    xs = jnp.pad(x_sample, ((0, 0), (0, rows - ts), (0, 0))).reshape(bs * rows, d)
    (q, kk, v, lf, gh, ak_s, av_s, ag, ik_s, aq, iq, iw) = _project(
        xs, wm, wt, wx, lb_logits, tm=bs * rows, kc=0)
    rs = lambda a: a.reshape(bs, rows, a.shape[-1])
    o_h, s_s = _hgrn(rs(q), rs(kk), rs(v), rs(lf), state_hgrn[0], rows=rows, t_valid=ts)
    nsel = min(TOPK_MAX, (n_pages * page + ts) // 4)
    kidx_t = jnp.swapaxes(cache_kidx[0], 1, 2)
    scores, thr = _sample_scores(page_table, kidx_t, rs(iq), rs(iw), rs(ik_s), nsel=nsel, t_valid=ts)
    pages = lambda c: c[0].reshape(c.shape[1], page * H_ATT, D_HEAD)
    o_a = _sample_attn(page_table, pages(cache_k), pages(cache_v), rs(aq), rs(ak_s), rs(av_s), scores, thr)
    y_s = _merge(xs, o_h.reshape(bs * rows, SEG), gh, o_a.reshape(bs * rows, SEG), ag,
                 gnh, gna, wo, lng, lnb, tm=bs * rows)

    heads = lambda a, n: a.reshape(1, 1, n, H_ATT, D_HEAD)
    sample = lambda a: a.reshape(bs, rows, -1)[:, :ts]
    return (y_p.reshape(1, t, d),
            sample(y_s),
            heads(ak, t), heads(av, t), ik.reshape(1, 1, t, D_IDX), s_p[None],
            sample(ak_s).reshape(1, bs, ts, H_ATT, D_HEAD), sample(av_s).reshape(1, bs, ts, H_ATT, D_HEAD),
            sample(ik_s)[None], s_s[None])
```

```python
import functools

import jax
import jax.numpy as jnp
from jax import lax
from jax.experimental import pallas as pl
from jax.experimental.pallas import tpu as pltpu

F32 = jnp.float32
BF16 = jnp.bfloat16
I32 = jnp.int32

H_HGRN = 4
H_ATT = 4
D_HEAD = 128
H_IDX = 8
D_IDX = 64
SEG = 512
N_SEG = 9
TOPK_MAX = 256
HGRN_CHUNK = 64
Q_BLOCK = 256
PROMPT_KEY_CHUNK = 512
EPS = 1e-5
DEPTH = 1
ALPHA = (2 * DEPTH) ** 0.25
IDX_SCALE = H_IDX ** -0.5 * D_IDX ** -0.5
ATT_SCALE = D_HEAD ** -0.5
LOG2E = 1.4426950408889634

LANES = 128
SUBLANES = 8
SAMPLE_ROWS = SUBLANES

INT_MAX = 2 ** 31 - 1
KEY_NEG_INF = -2139095041
KEY_POS_INF = 2139095040
ZERO_KEY_MIN = -(2 ** 23)
ZERO_KEY_MAX = 2 ** 23 - 1
NEG_BIG = -1e30
MAX_BISECT_STEPS = 34
TIE_GROUP_KEYS = 1

SCORE_PAGES_PER_GROUP = 32
SCORE_PAGES_PER_DOT = 8
ATTN_PAGES_PER_GROUP = 8
PAGE_SLOTS = 4

NT_DIMS = (((1,), (1,)), ((), ()))
TN_DIMS = (((0,), (0,)), ((), ()))


def _sigmoid(x):
    return 1.0 / (1.0 + jnp.exp(-x))


def _vmem_params(nbytes, semantics):
    return pltpu.CompilerParams(dimension_semantics=semantics, vmem_limit_bytes=int(nbytes))


def _resident(shape):
    zeros = (0,) * len(shape)
    return pl.BlockSpec(shape, lambda *_: zeros, pipeline_mode=pl.Buffered(1))


def _lane_sum(x):
    return jnp.broadcast_to(jnp.sum(x, axis=1, keepdims=True), x.shape)


def _sublane_all(op, x):
    return jnp.broadcast_to(op(x, axis=0, keepdims=True), x.shape)


def _pad_rows(a, n):
    return jnp.concatenate([a, jnp.zeros((n - a.shape[0], a.shape[1]), a.dtype)], axis=0)


def _proj_kernel(x_ref, wm_ref, wt_ref, wx_ref, lbl_ref, *out_refs, kc):
    x = x_ref[...].astype(BF16)

    def seg(j):
        return jnp.dot(x, wm_ref[:, j * SEG:(j + 1) * SEG], preferred_element_type=F32)

    q_ref, kk_ref, v_ref, lf_ref, gh_ref, ak_ref, av_ref, ag_ref, ik_ref = out_refs[:9]
    hq = seg(0)
    q_ref[...] = hq * _sigmoid(hq)
    lbl = lbl_ref[...]
    e = jnp.exp(lbl - jnp.max(lbl, axis=0, keepdims=True))
    lb = e[0:1] / jnp.sum(e, axis=0, keepdims=True)
    f = lb + (1.0 - lb) * _sigmoid(seg(1))
    kk_ref[...] = 1.0 - f
    lf_ref[...] = jnp.log(f)
    v_ref[...] = seg(2)
    gh_ref[...] = seg(3)
    ak = seg(5)
    av = seg(6)
    if kc == 0:
        ak_ref[...] = ak
        av_ref[...] = av
    else:
        for h in range(H_ATT):
            ak_ref[pl.ds(h, ak.shape[0], stride=H_ATT), :] = ak[:, h * D_HEAD:(h + 1) * D_HEAD]
            av_ref[pl.ds(h, av.shape[0], stride=H_ATT), :] = av[:, h * D_HEAD:(h + 1) * D_HEAD]
    ag_ref[...] = seg(7)
    tail = jnp.dot(x, wt_ref[...], preferred_element_type=F32)
    ik = tail[:, :D_IDX]
    ik_ref[...] = ik
    if kc == 0:
        aq_ref, iq_ref, iw_ref = out_refs[9:]
        aq_ref[...] = seg(4)
        iq_ref[...] = seg(8)
        iw_ref[...] = tail[:, D_IDX:D_IDX + H_IDX]
    else:
        kbf_ref, kib_ref, aqt_ref, qit_ref, iwt_ref, vt_ref = out_refs[9:]
        kbf_ref[...] = ak.astype(BF16)
        kib_ref[...] = ik.astype(BF16)
        xt = lax.dot_general(wx_ref[...], x, NT_DIMS, preferred_element_type=F32)
        aqt_ref[...] = (xt[:SEG] * (ATT_SCALE * LOG2E)).astype(BF16)
        qit_ref[...] = xt[SEG:2 * SEG].astype(BF16)
        iwt_ref[...] = xt[3 * SEG:] * IDX_SCALE
        vt = xt[2 * SEG:3 * SEG].astype(BF16)
        for j in range(vt.shape[1] // kc):
            vt_ref[j] = vt[:, j * kc:(j + 1) * kc]


def _project(x2, wm, wt, wx, lbl, *, tm, kc):
    t, d = x2.shape
    assert t % tm == 0 and (kc == 0 or tm % kc == 0)
    row = lambda w: pl.BlockSpec((tm, w), lambda i: (i, 0))
    col = lambda h: pl.BlockSpec((h, tm), lambda i: (0, i))
    f32 = lambda *s: jax.ShapeDtypeStruct(s, F32)
    b16 = lambda *s: jax.ShapeDtypeStruct(s, BF16)
    out_shape = [f32(t, SEG)] * 8 + [f32(t, D_IDX)]
    out_specs = [row(SEG)] * 8 + [row(D_IDX)]
    if kc:
        out_shape[5:7] = [f32(t * H_ATT, D_HEAD)] * 2
        out_specs[5:7] = [pl.BlockSpec((tm * H_ATT, D_HEAD), lambda i: (i, 0))] * 2
    if kc == 0:
        out_shape += [f32(t, SEG), f32(t, SEG), f32(t, H_IDX)]
        out_specs += [row(SEG), row(SEG), row(H_IDX)]
    else:
        out_shape += [b16(t, SEG), b16(t, D_IDX), b16(SEG, t), b16(SEG, t), f32(H_IDX, t), b16(t // kc, SEG, kc)]
        out_specs += [row(SEG), row(D_IDX), col(SEG), col(SEG), col(H_IDX),
                      pl.BlockSpec((tm // kc, SEG, kc), lambda i: (i, 0, 0))]
    vmem = (wm.size + wx.size) * 2 + 2 * tm * d * 4 + 2 * tm * SEG * 12 * 4 + (8 << 20)
    return pl.pallas_call(
        functools.partial(_proj_kernel, kc=kc),
        grid=(t // tm,),
        in_specs=[row(d), _resident(wm.shape), _resident(wt.shape), _resident(wx.shape), _resident(lbl.shape)],
        out_specs=out_specs,
        out_shape=out_shape,
        compiler_params=_vmem_params(vmem, ("parallel",)),
        name="proj",
    )(x2, wm, wt, wx, lbl)


def _hgrn_kernel(q_ref, kk_ref, v_ref, lf_ref, s0_ref, o_ref, sout_ref, st_ref, *, rows, t_valid):
    c_len = HGRN_CHUNK
    step = pl.program_id(1)

    @pl.when(step == 0)
    def _():
        for h in range(H_HGRN):
            st_ref[h] = s0_ref[0, h].T

    r = lax.broadcasted_iota(I32, (c_len, c_len), 0)
    c = lax.broadcasted_iota(I32, (c_len, c_len), 1)
    causal = r >= c
    tri = causal.astype(F32)
    pad = c_len - rows if rows < c_len else 0

    def load(ref, r0, n, h):
        a = ref[0, r0:r0 + n, h * D_HEAD:(h + 1) * D_HEAD]
        if pad:
            a = jnp.concatenate([a, jnp.zeros((pad, D_HEAD), F32)], axis=0)
        return a

    n_rows = min(rows, c_len)
    n_chunks = max(rows // c_len, 1)
    tiles = [(ci, h) for ci in range(n_chunks) for h in range(H_HGRN)]

    def log_decay(ci, h):
        lf = load(lf_ref, ci * c_len, n_rows, h)
        if t_valid < c_len:
            lf = jnp.where(lax.broadcasted_iota(I32, lf.shape, 0) < t_valid, lf, 0.0)
        return jnp.dot(tri, lf, precision=lax.Precision.HIGHEST, preferred_element_type=F32)

    b = {k: log_decay(*k) for k in tiles}
    b_end = {k: b[k][c_len - 1:c_len, :] for k in tiles}
    qd = {(ci, h): (load(q_ref, ci * c_len, n_rows, h) * jnp.exp(b[ci, h])).astype(BF16) for ci, h in tiles}
    kk = {(ci, h): load(kk_ref, ci * c_len, n_rows, h) for ci, h in tiles}
    kd = {k: (kk[k] * jnp.exp(-b[k])).astype(BF16) for k in tiles}
    k2 = {k: (kk[k] * jnp.exp(b_end[k] - b[k])).astype(BF16) for k in tiles}
    vb = {(ci, h): load(v_ref, ci * c_len, n_rows, h).astype(BF16) for ci, h in tiles}
    att = {k: lax.dot_general(qd[k], kd[k], NT_DIMS, preferred_element_type=F32) for k in tiles}
    att = {k: jnp.where(causal, att[k], 0.0).astype(BF16) for k in tiles}
    kv = {k: lax.dot_general(vb[k], k2[k], TN_DIMS, preferred_element_type=F32) for k in tiles}
    o_intra = {k: jnp.dot(att[k], vb[k], preferred_element_type=F32) for k in tiles}
    pre = {k: (qd[k], o_intra[k], kv[k], jnp.exp(b_end[k])) for k in tiles}
    for h in range(H_HGRN):
        st = st_ref[h]
        for ci in range(n_chunks):
            qd, o_intra, kv, decay = pre[ci, h]
            o = o_intra + lax.dot_general(qd, st.astype(BF16), NT_DIMS, preferred_element_type=F32)
            o_ref[0, ci * c_len:ci * c_len + n_rows, h * D_HEAD:(h + 1) * D_HEAD] = o[:n_rows]
            st = st * decay + kv
        st_ref[h] = st

    @pl.when(step == pl.num_programs(1) - 1)
    def _():
        for h in range(H_HGRN):
            sout_ref[0, h] = st_ref[h].T


def _hgrn(q, kk, v, lf, s0, *, rows, t_valid):
    b, t, _ = q.shape
    assert t % rows == 0
    blk = pl.BlockSpec((1, rows, SEG), lambda i, j: (i, j, 0))
    sblk = pl.BlockSpec((1, H_HGRN, D_HEAD, D_HEAD), lambda i, j: (i, 0, 0, 0))
    return pl.pallas_call(
        functools.partial(_hgrn_kernel, rows=rows, t_valid=t_valid),
        grid=(b, t // rows),
        in_specs=[blk, blk, blk, blk, sblk],
        out_specs=[blk, sblk],
        out_shape=[jax.ShapeDtypeStruct((b, t, SEG), F32),
                   jax.ShapeDtypeStruct((b, H_HGRN, D_HEAD, D_HEAD), F32)],
        scratch_shapes=[pltpu.VMEM((H_HGRN, D_HEAD, D_HEAD), F32)],
        compiler_params=_vmem_params(32 << 20, ("parallel", "arbitrary")),
        name="hgrn",
    )(q, kk, v, lf, s0)


def _is_zero_key(k):
    return jnp.logical_and(k >= ZERO_KEY_MIN, k <= ZERO_KEY_MAX)


def _key_to_float(k):
    f = lax.bitcast_convert_type(k ^ (lax.shift_right_arithmetic(k, 31) & INT_MAX), F32)
    return jnp.where(_is_zero_key(k), 0.0, f)


def _next_key(k, step=1):
    return jnp.where(_is_zero_key(k), jnp.maximum(k + step, ZERO_KEY_MAX + 1), k + step)


def _tie_group(lo):
    g = TIE_GROUP_KEYS
    start = lo if g == 1 else jnp.where(lo > KEY_NEG_INF + g, lo & (-g), lo)
    return start, _next_key(jnp.minimum(start, KEY_POS_INF + 1 - g), g)


def _bisect_threshold(count_ge, nsel, lo0, hi0):
    def active(lo, hi):
        return (lo + 1) != hi

    def cond(state):
        it, lo, hi, _ = state
        return jnp.logical_and(it < MAX_BISECT_STEPS,
                               jnp.max(jnp.where(active(lo, hi), 1.0, 0.0)) > 0.0)

    def body(state):
        it, lo, hi, clo = state
        mid = (lo & hi) + lax.shift_right_arithmetic(lo ^ hi, 1)
        plateau_end = ZERO_KEY_MAX + 1
        on_plateau = _is_zero_key(mid)
        inside = jnp.logical_and(on_plateau, jnp.logical_and(hi <= plateau_end, lo >= ZERO_KEY_MIN))
        mid = jnp.where(on_plateau,
                        jnp.where(hi > plateau_end, plateau_end, jnp.where(lo < ZERO_KEY_MIN, ZERO_KEY_MIN, lo)),
                        mid)
        cnt = count_ge(_key_to_float(mid))
        ge = cnt >= nsel
        exact = jnp.logical_or(cnt == nsel, inside)
        act = active(lo, hi)
        new_hi = jnp.where(exact, mid + 1, jnp.where(ge, hi, mid))
        return (it + 1,
                jnp.where(jnp.logical_and(act, ge), mid, lo),
                jnp.where(act, new_hi, hi),
                jnp.where(jnp.logical_and(act, ge), cnt, clo))

    _, lo, _, clo = lax.while_loop(cond, body, (jnp.int32(0), lo0, hi0, jnp.zeros(lo0.shape, F32)))
    return lo, clo


def _topk_threshold_rows(sc_ref, nch, nsel, rows, rows_valid):
    shape = (rows, LANES)

    def count_ge(thr):
        parts = [jnp.zeros(shape, F32)] * 4
        for ci in range(nch):
            parts[ci % 4] = parts[ci % 4] + jnp.where(sc_ref[ci] >= thr, 1.0, 0.0)
        return _lane_sum((parts[0] + parts[1]) + (parts[2] + parts[3]))

    real = lax.broadcasted_iota(I32, shape, 0) < rows_valid
    lo, clo = _bisect_threshold(count_ge, nsel, jnp.full(shape, KEY_NEG_INF, I32),
                                jnp.where(real, KEY_POS_INF + 1, KEY_NEG_INF + 1))
    lo, group_end = _tie_group(lo)
    if TIE_GROUP_KEYS > 1:
        clo = jnp.where(real, count_ge(_key_to_float(lo)), 0.0)
    t = _key_to_float(lo)

    @pl.when(jnp.max(clo) > nsel)
    def _():
        t_next = _key_to_float(group_end)
        need = nsel - count_ge(t_next)
        zero = jnp.zeros(shape, F32)

        def residual(s):
            return jnp.where(jnp.logical_and(s >= t, s < t_next), s - t, jnp.nan)

        def count_res(pred):
            return _lane_sum(functools.reduce(
                jnp.add, [jnp.where(pred(residual(sc_ref[ci])), 1.0, 0.0) for ci in range(nch)]))

        key0 = jnp.zeros(shape, I32)
        lo2, _ = lax.cond(jnp.max(count_res(lambda r: r > 0.0)) > 0.0,
                          lambda: _bisect_threshold(lambda thr: count_res(lambda r: r >= thr), need, key0,
                                                    jnp.full(shape, KEY_POS_INF + 1, I32)),
                          lambda: (key0, zero))
        t2 = _key_to_float(lo2)
        t2_next = _key_to_float(_next_key(lo2))
        need2 = need - count_res(lambda r: r >= t2_next)
        strict_upper = (lax.broadcasted_iota(I32, (LANES, LANES), 0)
                        < lax.broadcasted_iota(I32, (LANES, LANES), 1)).astype(BF16)

        def tie_body(ci, seen):
            s = sc_ref[ci]
            res = residual(s)
            tied = jnp.logical_and(res >= t2, res < t2_next)
            tiedf = jnp.where(tied, 1.0, 0.0)
            before = seen + jnp.dot(tiedf.astype(BF16), strict_upper, preferred_element_type=F32)
            lose = jnp.logical_or(res < t2, jnp.logical_and(tied, before >= need2))
            sc_ref[ci] = jnp.where(lose, jnp.nan, s)
            return seen + _lane_sum(tiedf)

        lax.fori_loop(0, nch, tie_body, zero)

    return t


def _topk_threshold_cols(sc_ref, nch, nsel, kc):
    ql = sc_ref.shape[2]
    shape = (SUBLANES, ql)
    nv = kc // SUBLANES
    zero = jnp.zeros(shape, F32)

    def count_ge(thr):
        def body(ci, parts):
            parts = list(parts)
            for v in range(nv):
                s = sc_ref[ci, v * SUBLANES:(v + 1) * SUBLANES, :]
                parts[v % 4] = parts[v % 4] + jnp.where(s >= thr, 1.0, 0.0)
            return tuple(parts)
        parts = lax.fori_loop(0, nch, body, (zero,) * 4)
        return _sublane_all(jnp.sum, (parts[0] + parts[1]) + (parts[2] + parts[3]))

    def reduce_tiles(fn, op, init):
        def body(ci, parts):
            parts = list(parts)
            for v in range(nv):
                s = sc_ref[ci, v * SUBLANES:(v + 1) * SUBLANES, :]
                parts[v % 4] = op(parts[v % 4], fn(s))
            return tuple(parts)
        parts = lax.fori_loop(0, nch, body, (init,) * 4)
        return op(op(parts[0], parts[1]), op(parts[2], parts[3]))

    lo, clo = _bisect_threshold(count_ge, nsel, jnp.full(shape, KEY_NEG_INF, I32),
                                jnp.full(shape, KEY_POS_INF + 1, I32))
    lo, group_end = _tie_group(lo)
    if TIE_GROUP_KEYS > 1:
        clo = count_ge(_key_to_float(lo))
    t = _key_to_float(lo)

    @pl.when(jnp.max(clo) > nsel)
    def _():
        t_next = _key_to_float(group_end)
        need = nsel - count_ge(t_next)

        def residual(s, t_, t_next_):
            return jnp.where(jnp.logical_and(s >= t_, s < t_next_), s - t_, jnp.nan)

        def count_res_ge(thr):
            return _sublane_all(jnp.sum, reduce_tiles(
                lambda s: jnp.where(residual(s, t, t_next) >= thr, 1.0, 0.0), jnp.add, zero))

        rmax = _sublane_all(jnp.max, reduce_tiles(
            lambda s: jnp.where(residual(s, t, t_next) > 0.0, 1.0, 0.0), jnp.maximum, zero))
        key0 = jnp.zeros(shape, I32)
        lo2, _ = lax.cond(jnp.max(rmax) > 0.0,
                          lambda: _bisect_threshold(count_res_ge, need, key0, jnp.full(shape, KEY_POS_INF + 1, I32)),
                          lambda: (key0, zero))
        t2 = _key_to_float(lo2)
        t2_next = _key_to_float(_next_key(lo2))
        need2 = (need - count_res_ge(t2_next))[0:1]
        t_r, tn_r, t2_r, t2n_r = t[0:1], t_next[0:1], t2[0:1], t2_next[0:1]
        strict_lower = (lax.broadcasted_iota(I32, (kc, kc), 1)
                        < lax.broadcasted_iota(I32, (kc, kc), 0)).astype(BF16)

        def tie_body(ci, seen):
            s = sc_ref[ci]
            res = residual(s, t_r, tn_r)
            tied = jnp.logical_and(res >= t2_r, res < t2n_r)
            tiedf = jnp.where(tied, 1.0, 0.0)
            before = seen + jnp.dot(strict_lower, tiedf.astype(BF16), preferred_element_type=F32)
            lose = jnp.logical_or(res < t2_r, jnp.logical_and(tied, before >= need2))
            sc_ref[ci] = jnp.where(lose, jnp.nan, s)
            return seen + jnp.sum(tiedf, axis=0, keepdims=True)

        lax.fori_loop(0, nch, tie_body, jnp.zeros((1, ql), F32))

    return t


def _dsa_prompt_kernel(aqt_ref, qit_ref, iwt_ref, kbf_ref, kib_ref, vt_ref, o_ref, sc_ref, acc_ref, *, nsel, kc):
    qb = ql = o_ref.shape[0]
    vc = vt_ref.shape[2]
    nv = kc // SUBLANES
    i = pl.program_id(0)
    nch = 2 * (((i + 1) * qb + 2 * kc - 1) // (2 * kc))
    n_full = (i * qb + 1) // kc

    qit = qit_ref[...]
    rhs = jnp.concatenate([qit[h * D_IDX:(h + 1) * D_IDX, :] for h in range(H_IDX)], axis=1)
    w = iwt_ref[...]

    def scores(ci):
        parts = []
        for j in range(kc // vc):
            r0 = pl.multiple_of(ci * kc + j * vc, vc)
            s = jnp.dot(kib_ref[pl.ds(r0, vc), :], rhs, preferred_element_type=F32)
            acc = jnp.zeros((vc, ql), F32)
            for h in range(H_IDX):
                acc = acc + jnp.maximum(s[:, h * ql:(h + 1) * ql], 0.0) * w[h:h + 1, :]
            parts.append(acc)
        return jnp.concatenate(parts, axis=0)

    def score_body(ci, carry):
        sc_ref[ci] = scores(ci)
        return carry

    def masked_score_body(ci, carry):
        kpos = ci * kc + lax.broadcasted_iota(I32, (kc, ql), 0)
        qpos = i * qb + lax.broadcasted_iota(I32, (kc, ql), 1)
        sc_ref[ci] = jnp.where(kpos <= qpos, scores(ci), jnp.nan)
        return carry

    lax.fori_loop(0, n_full, score_body, 0)
    lax.fori_loop(n_full, nch, masked_score_body, 0)

    t = _topk_threshold_cols(sc_ref, nch, nsel, kc)

    aqt = aqt_ref[...]
    qh = [aqt[h * D_HEAD:(h + 1) * D_HEAD, :] for h in range(H_ATT)]
    zq = jnp.zeros((D_HEAD, ql), BF16)
    qpair = [jnp.concatenate([jnp.concatenate([qh[2 * p], zq], axis=1),
                              jnp.concatenate([zq, qh[2 * p + 1]], axis=1)], axis=0) for p in range(H_ATT // 2)]
    for h in range(H_ATT):
        acc_ref[h] = jnp.zeros((D_HEAD, ql), F32)

    def tile(x):
        return jnp.broadcast_to(x[None], (nv, SUBLANES, ql)).reshape(kc, ql)

    heads = range(H_ATT)

    def logits(ci):
        r0 = pl.multiple_of(ci * kc, kc)
        return [jnp.dot(kbf_ref[pl.ds(r0, kc), p * 2 * D_HEAD:(p + 1) * 2 * D_HEAD], qpair[p],
                        preferred_element_type=F32) for p in range(H_ATT // 2)]

    def softmax_step(ci, lg_pair, carry):
        m, l = carry
        bias = jnp.where(sc_ref[ci] >= tile(t), 0.0, NEG_BIG)
        lg = [lg_pair[h // 2][:, (h % 2) * ql:(h % 2 + 1) * ql] + bias for h in heads]
        mx = [jnp.max(lg[h].reshape(nv, SUBLANES, ql), axis=0) for h in heads]
        m_new = [jnp.maximum(m[h], _sublane_all(jnp.max, mx[h])) for h in heads]
        alpha = [jnp.exp2(m[h] - m_new[h]) for h in heads]
        p = [jnp.exp2(lg[h] - tile(m_new[h])) for h in heads]
        l_new = [alpha[h] * l[h] + jnp.sum(p[h].reshape(nv, SUBLANES, ql), axis=0) for h in heads]
        return (m_new, l_new), alpha, [p[h].astype(BF16) for h in heads]

    def accumulate(ci, alpha, pb):
        pv = [functools.reduce(jnp.add, [
            jnp.dot(vt_ref[ci * (kc // vc) + j, h * D_HEAD:(h + 1) * D_HEAD, :], pb[h][j * vc:(j + 1) * vc],
                    preferred_element_type=F32) for j in range(kc // vc)]) for h in heads]
        for h in heads:
            acc_ref[h] = acc_ref[h] * jnp.broadcast_to(alpha[h][0:1], (D_HEAD, ql)) + pv[h]

    def att_body(pi, carry):
        ca, cb = 2 * pi, 2 * pi + 1
        lg_a = logits(ca)
        lg_b = logits(cb)
        carry, alpha_a, pb_a = softmax_step(ca, lg_a, carry)
        accumulate(ca, alpha_a, pb_a)
        carry, alpha_b, pb_b = softmax_step(cb, lg_b, carry)
        accumulate(cb, alpha_b, pb_b)
        return carry

    init = ([jnp.full((SUBLANES, ql), NEG_BIG, F32)] * H_ATT, [jnp.zeros((SUBLANES, ql), F32)] * H_ATT)
    _, l = lax.fori_loop(0, nch // 2, att_body, init)
    for h in range(H_ATT):
        denom = jnp.broadcast_to(jnp.sum(l[h], axis=0, keepdims=True), (D_HEAD, ql))
        o_ref[:, h * D_HEAD:(h + 1) * D_HEAD] = (acc_ref[h] / denom).T


def _dsa_prompt(aqt, qit, iwt, kbf, kib, vt, *, nsel):
    t = kbf.shape[0]
    vc = vt.shape[2]
    kc = PROMPT_KEY_CHUNK
    nchunks = t // kc
    qb = Q_BLOCK
    assert t % qb == 0 and t % (2 * kc) == 0 and kc % vc == 0 and vt.shape[0] * vc == t
    col = lambda h: pl.BlockSpec((h, qb), lambda i: (0, i))
    vmem = kbf.size * 2 * 2 + t * LANES * 2 + nchunks * kc * qb * 4 + (6 << 20)
    return pl.pallas_call(
        functools.partial(_dsa_prompt_kernel, nsel=nsel, kc=kc),
        grid=(t // qb,),
        in_specs=[col(SEG), col(SEG), col(H_IDX), _resident(kbf.shape), _resident(kib.shape), _resident(vt.shape)],
        out_specs=pl.BlockSpec((qb, SEG), lambda i: (i, 0)),
        out_shape=jax.ShapeDtypeStruct((t, SEG), F32),
        scratch_shapes=[pltpu.VMEM((nchunks, kc, qb), F32), pltpu.VMEM((H_ATT, D_HEAD, qb), F32)],
        compiler_params=_vmem_params(vmem, ("arbitrary",)),
        name="dsa_prompt",
    )(aqt, qit, iwt, kbf, kib, vt)


def _page_group_copies(hbm_refs, bufs, sems, pt_ref, seq, group, slot, pages_per_group):
    copies = []
    for hbm, buf, sem in zip(hbm_refs, bufs, sems):
        for j in range(pages_per_group):
            page = pt_ref[seq, group * pages_per_group + j]
            copies.append(pltpu.make_async_copy(hbm.at[page], buf.at[slot, j], sem.at[slot]))
    return copies


def _stream_page_groups(hbm_refs, bufs, sems, pt_ref, n_groups, pages_per_group, consume, carry):
    n_slots = PAGE_SLOTS
    ahead = n_slots - 1
    assert n_groups % n_slots == 0 and ahead <= n_groups
    b = pl.program_id(0)
    n_seq = pl.num_programs(0)
    copies = functools.partial(_page_group_copies, hbm_refs, bufs, sems, pt_ref,
                               pages_per_group=pages_per_group)

    @pl.when(b == 0)
    def _():
        for g0 in range(ahead):
            for c in copies(0, g0, g0):
                c.start()

    def ring_body(i, carry):
        for slot in range(n_slots):
            g = n_slots * i + slot
            nxt = g + ahead
            nxt_slot = (slot + ahead) % n_slots

            @pl.when(nxt < n_groups)
            def _():
                for c in copies(b, nxt, nxt_slot):
                    c.start()

            @pl.when(jnp.logical_and(nxt >= n_groups, b + 1 < n_seq))
            def _():
                for c in copies(b + 1, nxt - n_groups, nxt_slot):
                    c.start()

            for c in copies(b, g, slot):
                c.wait()
            carry = consume(g, slot, carry)
        return carry

    return lax.fori_loop(0, n_groups // n_slots, ring_body, carry)


def _sample_scores_kernel(pt_ref, kidx_hbm, qi_ref, wi_ref, kin_ref, sc_ref, thr_ref, kbuf, sem,
                          *, n_pages, nsel, t_valid):
    rows = SAMPLE_ROWS
    gp, dp = SCORE_PAGES_PER_GROUP, SCORE_PAGES_PER_DOT
    qi = qi_ref[0]
    qhm = jnp.concatenate([qi[:, h * D_IDX:(h + 1) * D_IDX] for h in range(H_IDX)], axis=0).astype(BF16)
    wi = wi_ref[0] * IDX_SCALE

    def head_sum(s):
        acc = jnp.zeros((rows, s.shape[1]), F32)
        for h in range(H_IDX):
            acc = acc + jnp.maximum(s[h * rows:(h + 1) * rows], 0.0) * wi[:, h:h + 1]
        return acc

    def consume(g, slot, carry):
        for d in range(gp // dp):
            kt = jnp.concatenate([kbuf[slot, d * dp + j] for j in range(dp)], axis=1).astype(BF16)
            acc = head_sum(jnp.dot(qhm, kt, preferred_element_type=F32))
            for j in range(dp):
                sc_ref[0, g * gp + d * dp + j] = acc[:, j * LANES:(j + 1) * LANES]
        return carry

    _stream_page_groups([kidx_hbm], [kbuf], [sem], pt_ref, n_pages // gp, gp, consume, 0)

    kn = _pad_rows(kin_ref[0], LANES).astype(BF16)
    r = lax.broadcasted_iota(I32, (rows, LANES), 0)
    c = lax.broadcasted_iota(I32, (rows, LANES), 1)
    acc = head_sum(lax.dot_general(qhm, kn, NT_DIMS, preferred_element_type=F32))
    sc_ref[0, n_pages] = jnp.where(jnp.logical_and(c <= r, c < t_valid), acc, jnp.nan)
    thr_ref[0] = _topk_threshold_rows(sc_ref.at[0], n_pages + 1, nsel, rows, t_valid)


def _sample_scores(page_table, kidx_t, qi, wi, ki_new, *, nsel, t_valid):
    b, n_pages = page_table.shape
    _, _, page = kidx_t.shape
    gp = SCORE_PAGES_PER_GROUP
    assert page == LANES and n_pages % (PAGE_SLOTS * gp) == 0
    rows = SAMPLE_ROWS
    seq = lambda *tail: pl.BlockSpec((1,) + tail, lambda i, pt: (i,) + (0,) * len(tail))
    grid_spec = pltpu.PrefetchScalarGridSpec(
        num_scalar_prefetch=1,
        grid=(b,),
        in_specs=[pl.BlockSpec(memory_space=pl.ANY), seq(rows, SEG), seq(rows, H_IDX), seq(rows, D_IDX)],
        out_specs=[seq(n_pages + 1, rows, page), seq(rows, LANES)],
        scratch_shapes=[pltpu.VMEM((PAGE_SLOTS, gp, D_IDX, page), F32), pltpu.SemaphoreType.DMA((PAGE_SLOTS,))],
    )
    return pl.pallas_call(
        functools.partial(_sample_scores_kernel, n_pages=n_pages, nsel=nsel, t_valid=t_valid),
        grid_spec=grid_spec,
        out_shape=[jax.ShapeDtypeStruct((b, n_pages + 1, rows, page), F32),
                   jax.ShapeDtypeStruct((b, rows, LANES), F32)],
        compiler_params=_vmem_params(32 << 20, ("arbitrary",)),
        name="sample_scores",
    )(page_table, kidx_t, qi, wi, ki_new)


def _sample_attn_kernel(pt_ref, k_hbm, v_hbm, aq_ref, kn_ref, vn_ref, sc_ref, thr_ref, o_ref,
                        kbuf, vbuf, ksem, vsem, *, n_pages):
    rows = SAMPLE_ROWS
    gp = ATTN_PAGES_PER_GROUP
    page = LANES
    thr = thr_ref[0][:, 0:1]
    aq = aq_ref[0]
    qh = [aq[:, h * D_HEAD:(h + 1) * D_HEAD].astype(BF16) for h in range(H_ATT)]

    def head_rows(buf, slot, j, h):
        return buf[slot, j, pl.ds(h, page, stride=H_ATT), :].astype(BF16)

    def attend(carry, scores, keys, values):
        m, l, acc = carry
        sel = scores >= thr
        heads = range(H_ATT)
        lg = [lax.dot_general(qh[h], keys[h], NT_DIMS, preferred_element_type=F32) for h in heads]
        lg = [jnp.where(sel, lg[h] * ATT_SCALE, NEG_BIG) for h in heads]
        m_new = [jnp.maximum(m[h], jnp.max(lg[h], axis=1, keepdims=True)) for h in heads]
        alpha = [jnp.exp(m[h] - m_new[h]) for h in heads]
        p = [jnp.where(sel, jnp.exp(lg[h] - m_new[h]), 0.0) for h in heads]
        pv = [jnp.dot(p[h].astype(BF16), values[h], preferred_element_type=F32) for h in heads]
        return (m_new, [alpha[h] * l[h] + jnp.sum(p[h], axis=1, keepdims=True) for h in heads],
                [alpha[h] * acc[h] + pv[h] for h in heads])

    def consume(g, slot, carry):
        scores = jnp.concatenate([sc_ref[0, g * gp + j] for j in range(gp)], axis=1)
        keys = [jnp.concatenate([head_rows(kbuf, slot, j, h) for j in range(gp)], axis=0) for h in range(H_ATT)]
        values = [jnp.concatenate([head_rows(vbuf, slot, j, h) for j in range(gp)], axis=0) for h in range(H_ATT)]
        return attend(carry, scores, keys, values)

    init = ([jnp.full((rows, 1), NEG_BIG, F32)] * H_ATT, [jnp.zeros((rows, 1), F32)] * H_ATT,
            [jnp.zeros((rows, LANES), F32)] * H_ATT)
    carry = _stream_page_groups([k_hbm, v_hbm], [kbuf, vbuf], [ksem, vsem], pt_ref, n_pages // gp, gp,
                                consume, init)
    kn, vn = kn_ref[0], vn_ref[0]
    hs = lambda h: slice(h * D_HEAD, (h + 1) * D_HEAD)
    _, l, acc = attend(carry, sc_ref[0, n_pages],
                       [_pad_rows(kn[:, hs(h)], page).astype(BF16) for h in range(H_ATT)],
                       [_pad_rows(vn[:, hs(h)], page).astype(BF16) for h in range(H_ATT)])
    for h in range(H_ATT):
        o_ref[0, :, hs(h)] = acc[h] / l[h]


def _sample_attn(page_table, cache_k, cache_v, aq, k_new, v_new, scores, thr):
    b, n_pages = page_table.shape
    gp = ATTN_PAGES_PER_GROUP
    rows = SAMPLE_ROWS
    page_rows = cache_k.shape[1]
    assert page_rows == LANES * H_ATT and n_pages % (PAGE_SLOTS * gp) == 0
    seq = lambda *tail: pl.BlockSpec((1,) + tail, lambda i, pt: (i,) + (0,) * len(tail))
    buf = pltpu.VMEM((PAGE_SLOTS, gp, page_rows, D_HEAD), F32)
    dma_sems = pltpu.SemaphoreType.DMA((PAGE_SLOTS,))
    grid_spec = pltpu.PrefetchScalarGridSpec(
        num_scalar_prefetch=1,
        grid=(b,),
        in_specs=[pl.BlockSpec(memory_space=pl.ANY), pl.BlockSpec(memory_space=pl.ANY),
                  seq(rows, SEG), seq(rows, SEG), seq(rows, SEG), seq(n_pages + 1, rows, LANES), seq(rows, LANES)],
        out_specs=seq(rows, SEG),
        scratch_shapes=[buf, buf, dma_sems, dma_sems],
    )
    vmem = 2 * PAGE_SLOTS * gp * page_rows * D_HEAD * 4 + (16 << 20)
    return pl.pallas_call(
        functools.partial(_sample_attn_kernel, n_pages=n_pages),
        grid_spec=grid_spec,
        out_shape=jax.ShapeDtypeStruct((b, rows, SEG), F32),
        compiler_params=_vmem_params(vmem, ("arbitrary",)),
        name="sample_attn",
    )(page_table, cache_k, cache_v, aq, k_new, v_new, scores, thr)


def _merge_kernel(x_ref, oh_ref, gh_ref, oa_ref, ga_ref, gnh_ref, gna_ref, wo_ref, lng_ref, lnb_ref, y_ref):
    def branch(o_ref, gate_ref, gain_ref):
        parts = []
        for h in range(H_ATT):
            o = o_ref[:, h * D_HEAD:(h + 1) * D_HEAD]
            parts.append(o * lax.rsqrt(jnp.mean(o * o, axis=-1, keepdims=True) + EPS))
        g = gate_ref[...]
        return (jnp.concatenate(parts, axis=1) * gain_ref[...]) * (g * _sigmoid(g))

    hh = branch(oh_ref, gh_ref, gnh_ref).astype(BF16)
    aa = branch(oa_ref, ga_ref, gna_ref).astype(BF16)
    mix = (jnp.dot(hh, wo_ref[:SEG, :], preferred_element_type=F32)
           + jnp.dot(aa, wo_ref[SEG:, :], preferred_element_type=F32))
    r = ALPHA * x_ref[...] + mix
    mu = jnp.mean(r, axis=-1, keepdims=True)
    rc = r - mu
    var = jnp.mean(rc * rc, axis=-1, keepdims=True)
    y_ref[...] = rc * lax.rsqrt(var + EPS) * lng_ref[...] + lnb_ref[...]


def _merge(x2, oh, gh, oa, ga, gnh, gna, wo, lng, lnb, *, tm):
    t, d = x2.shape
    assert t % tm == 0
    row = lambda w: pl.BlockSpec((tm, w), lambda i: (i, 0))
    return pl.pallas_call(
        _merge_kernel,
        grid=(t // tm,),
        in_specs=[row(d), row(SEG), row(SEG), row(SEG), row(SEG), _resident(gnh.shape), _resident(gna.shape),
                  _resident(wo.shape), _resident(lng.shape), _resident(lnb.shape)],
        out_specs=row(d),
        out_shape=jax.ShapeDtypeStruct((t, d), F32),
        compiler_params=_vmem_params(40 << 20, ("parallel",)),
        name="merge",
    )(x2, oh, gh, oa, ga, gnh, gna, wo, lng, lnb)


def kernel(x_prompt, x_sample, cache_k, cache_v, cache_kidx, state_hgrn, page_table,
           w_in, w_out, lb_logits, g_hgrn, g_attn, ln_g, ln_b):
    bp, t, d = x_prompt.shape
    bs, ts, _ = x_sample.shape
    n_pages = page_table.shape[1]
    page = cache_k.shape[2]
    assert w_in.shape[0] == DEPTH and bp == 1 and ts <= SAMPLE_ROWS
    main = N_SEG * SEG

    w = w_in[0]
    wm = w[:, :main].astype(BF16)
    wt = jnp.pad(w[:, main:], ((0, 0), (0, LANES - (D_IDX + H_IDX)))).astype(BF16)
    seg_cols = lambda j: w[:, j * SEG:(j + 1) * SEG]
    wx = jnp.concatenate([seg_cols(4), seg_cols(8), seg_cols(6), w[:, main + D_IDX:]], axis=1).T.astype(BF16)
    wo = w_out[0].astype(BF16)
    gnh, gna = g_hgrn[0][None, :], g_attn[0][None, :]
    lng, lnb = ln_g[0][None, :], ln_b[0][None, :]

    xp = x_prompt.reshape(t, d)
    kc = 2 * LANES
    (q, kk, v, lf, gh, ak, av, ag, ik, kbf, kib, aqt, qit, iwt, vt) = _project(
        xp, wm, wt, wx, lb_logits, tm=256, kc=kc)
    rs = lambda a: a.reshape(1, t, SEG)
    s0 = jnp.zeros((1, H_HGRN, D_HEAD, D_HEAD), F32)
    o_h, s_p = _hgrn(rs(q), rs(kk), rs(v), rs(lf), s0, rows=256, t_valid=HGRN_CHUNK)
    o_a = _dsa_prompt(aqt, qit, iwt, kbf, kib, vt, nsel=min(TOPK_MAX, t // 4))
    y_p = _merge(xp, o_h.reshape(t, SEG), gh, o_a, ag, gnh, gna, wo, lng, lnb, tm=512)

    rows = SAMPLE_ROWS
    xs = jnp.pad(x_sample, ((0, 0), (0, rows - ts), (0, 0))).reshape(bs * rows, d)
    (q, kk, v, lf, gh, ak_s, av_s, ag, ik_s, aq, iq, iw) = _project(
        xs, wm, wt, wx, lb_logits, tm=bs * rows, kc=0)
    rs = lambda a: a.reshape(bs, rows, a.shape[-1])
    o_h, s_s = _hgrn(rs(q), rs(kk), rs(v), rs(lf), state_hgrn[0], rows=rows, t_valid=ts)
    nsel = min(TOPK_MAX, (n_pages * page + ts) // 4)
    kidx_t = jnp.swapaxes(cache_kidx[0], 1, 2)
    scores, thr = _sample_scores(page_table, kidx_t, rs(iq), rs(iw), rs(ik_s), nsel=nsel, t_valid=ts)
    pages = lambda c: c[0].reshape(c.shape[1], page * H_ATT, D_HEAD)
    o_a = _sample_attn(page_table, pages(cache_k), pages(cache_v), rs(aq), rs(ak_s), rs(av_s), scores, thr)
    y_s = _merge(xs, o_h.reshape(bs * rows, SEG), gh, o_a.reshape(bs * rows, SEG), ag,
                 gnh, gna, wo, lng, lnb, tm=bs * rows)

    heads = lambda a, n: a.reshape(1, 1, n, H_ATT, D_HEAD)
    sample = lambda a: a.reshape(bs, rows, -1)[:, :ts]
    return (y_p.reshape(1, t, d),
            sample(y_s),
            heads(ak, t), heads(av, t), ik.reshape(1, 1, t, D_IDX), s_p[None],
            sample(ak_s).reshape(1, bs, ts, H_ATT, D_HEAD), sample(av_s).reshape(1, bs, ts, H_ATT, D_HEAD),
            sample(ik_s)[None], s_s[None])
```
